```python
import math, functools
import jax, jax.numpy as jnp
from jax import lax
import numpy as np

D_MODEL = 1024
BATCH = 2
SEQ = 8192
DEPTH = 4
DEC_BATCH = 128
DEC_SEQ = 4
PAST_LEN = 2048
PAGE_SIZE = 128

N_HEADS = 8
HEAD_DIM = 64
ATT_WIDTH = N_HEADS * HEAD_DIM
CONV_WIDTH = 512
CONV_K = 3
BLOCK = 256
TOP_K = 3
N_BUCKETS = 32
MAX_DIST = 1024
D_FF = 2816
FFN_K = 3
Q_CHUNK = 64
EPS = 1e-6
NEG = -1e30
N_IN = 3 * ATT_WIDTH + 3 * CONV_WIDTH + 2 * D_MODEL
SPLITS = (ATT_WIDTH, 2 * ATT_WIDTH, 3 * ATT_WIDTH,
          3 * ATT_WIDTH + CONV_WIDTH, 3 * ATT_WIDTH + 2 * CONV_WIDTH,
          3 * ATT_WIDTH + 3 * CONV_WIDTH, 3 * ATT_WIDTH + 3 * CONV_WIDTH + D_MODEL)

kernel_name = 'moba_shortconv_convffn_hybrid_step'


def rmsnorm(x, g):
    x32 = x.astype(jnp.float32)
    y = x32 * lax.rsqrt(jnp.mean(x32 * x32, axis=-1, keepdims=True) + EPS)
    return (y * g.astype(jnp.float32)).astype(x.dtype)


def causal_dwconv(u, prev, w, b=None):
    T = u.shape[1]
    K = w.shape[0]
    ext = jnp.concatenate([prev.astype(u.dtype), u], axis=1)
    y = w[0] * ext[:, 0:T]
    for i in range(1, K):
        y = y + w[i] * ext[:, i:i + T]
    if b is not None:
        y = y + b
    return y, ext[:, T:]


def rel_bucket(dist):
    n = jnp.maximum(dist, 0)
    max_exact = N_BUCKETS // 2
    nf = jnp.maximum(n, 1).astype(jnp.float32)
    large = max_exact + (jnp.log(nf / max_exact) / math.log(MAX_DIST / max_exact)
                         * (N_BUCKETS - max_exact)).astype(jnp.int32)
    large = jnp.minimum(large, N_BUCKETS - 1)
    return jnp.where(n < max_exact, n, large)


def to_blocks(k):
    B, L, H, dh = k.shape
    nb = -(-L // BLOCK)
    k = jnp.pad(k, ((0, 0), (0, nb * BLOCK - L), (0, 0), (0, 0)))
    return k.reshape(B, nb, BLOCK, H, dh).transpose(0, 3, 1, 2, 4)


def moba_core(q, qpos, kb, vb, kmean, rel_bias):
    H, NB = kb.shape[0], kb.shape[1]
    Q = q.shape[0]
    n_sel = min(TOP_K, NB)
    qblk = qpos // BLOCK
    gate = jnp.einsum('qhd,hnd->qhn', q.astype(jnp.float32), kmean)
    fully_past = jnp.arange(NB, dtype=jnp.int32)[None, None, :] < qblk[:, None, None]
    gate = jnp.where(fully_past, gate, NEG)
    _, top = lax.top_k(gate, n_sel)
    own = jnp.broadcast_to(qblk[:, None, None], (Q, H, 1))
    idx = jnp.concatenate([top, own], axis=-1)
    valid = jnp.concatenate([top < qblk[:, None, None], jnp.ones((Q, H, 1), bool)], axis=-1)
    hidx = jnp.arange(H)[None, :, None]
    k_sel = kb[hidx, idx]
    v_sel = vb[hidx, idx]
    logits = jnp.einsum('qhd,qhspd->qhsp', q, k_sel,
                        preferred_element_type=jnp.float32) * (HEAD_DIM ** -0.5)
    kpos = idx[..., None] * BLOCK + jnp.arange(BLOCK, dtype=jnp.int32)
    dist = qpos[:, None, None, None] - kpos
    bias = rel_bias.astype(jnp.float32)[rel_bucket(dist), jnp.arange(H)[None, :, None, None]]
    mask = valid[..., None] & (dist >= 0)
    logits = jnp.where(mask, logits + bias, NEG)
    p = jax.nn.softmax(logits.reshape(Q, H, -1), axis=-1).reshape(logits.shape)
    out = jnp.einsum('qhsp,qhspd->qhd', p.astype(v_sel.dtype), v_sel,
                     preferred_element_type=jnp.float32)
    return out.astype(q.dtype)


def moba_prompt(q, k, v, rel_bias):
    B, T, H, dh = q.shape
    kb, vb = to_blocks(k), to_blocks(v)
    kmean = jnp.mean(kb.astype(jnp.float32), axis=3)
    nc = T // Q_CHUNK
    q_ch = q.reshape(B, nc, Q_CHUNK, H, dh).transpose(1, 0, 2, 3, 4)
    pos_ch = jnp.arange(T, dtype=jnp.int32).reshape(nc, Q_CHUNK)
    core_b = jax.vmap(moba_core, in_axes=(0, None, 0, 0, 0, None))
    out = lax.map(lambda a: core_b(a[0], a[1], kb, vb, kmean, rel_bias), (q_ch, pos_ch))
    return out.transpose(1, 0, 2, 3, 4).reshape(B, T, H, dh)


def moba_sample(q, k, v, past_k, past_v, rel_bias):
    P, T = past_k.shape[1], q.shape[1]
    kb = to_blocks(jnp.concatenate([past_k.astype(k.dtype), k], axis=1))
    vb = to_blocks(jnp.concatenate([past_v.astype(v.dtype), v], axis=1))
    kmean = jnp.mean(kb.astype(jnp.float32), axis=3)
    qpos = P + jnp.arange(T, dtype=jnp.int32)
    return lax.map(lambda a: moba_core(a[0], qpos, a[1], a[2], a[3], rel_bias), (q, kb, vb, kmean))


def trunk_layer(x, c, attend, conv_prev, ffn_prev, ada_w, ada_b, g_mix, g_ffn, w_in, conv_w,
                w_att_out, w_conv_out, w_o, w_up, ffn_conv_w, ffn_conv_b, w_down):
    B, T, _ = x.shape
    mod = (jax.nn.silu(c) @ ada_w + ada_b)[:, None, :]
    sh1, sc1, gt1, sh2, sc2, gt2 = jnp.split(mod, 6, axis=-1)
    h = rmsnorm(x, g_mix) * (1 + sc1) + sh1
    proj = h @ w_in
    q, k, v, u, b_gate, c_gate, g_a, g_b = jnp.split(proj, SPLITS, axis=-1)
    q = q.reshape(B, T, N_HEADS, HEAD_DIM)
    k = k.reshape(B, T, N_HEADS, HEAD_DIM)
    v = v.reshape(B, T, N_HEADS, HEAD_DIM)
    att = attend(q, k, v).reshape(B, T, ATT_WIDTH)
    conv_out, conv_state = causal_dwconv(c_gate * u, conv_prev, conv_w)
    y_a = att @ w_att_out
    y_b = (b_gate * conv_out) @ w_conv_out
    merged = jax.nn.sigmoid(g_a) * y_a + jax.nn.sigmoid(g_b) * y_b
    x = x + gt1 * (merged @ w_o)
    h2 = rmsnorm(x, g_ffn) * (1 + sc2) + sh2
    up, ffn_state = causal_dwconv(h2 @ w_up, ffn_prev, ffn_conv_w, ffn_conv_b)
    a, bval = jnp.split(up, 2, axis=-1)
    x = x + gt2 * ((jax.nn.silu(a) * bval) @ w_down)
    return x, k, v, conv_state, ffn_state


def setup_inputs(seed: int = 0) -> dict:
    key = jax.random.key(seed)
    ks = jax.random.split(key, 24)
    nrm = jax.random.normal
    n_pages = PAST_LEN // PAGE_SIZE
    n_used = DEC_BATCH * n_pages
    n_phys = (5 * n_used) // 4
    page_table = jax.random.permutation(ks[8], n_phys)[:n_used].reshape(DEC_BATCH, n_pages).astype(jnp.int32)
    return {
        'x_prompt': nrm(ks[0], (BATCH, SEQ, D_MODEL), jnp.float32),
        'x_sample': nrm(ks[1], (DEC_BATCH, DEC_SEQ, D_MODEL), jnp.float32),
        'c_prompt': nrm(ks[2], (BATCH, D_MODEL), jnp.float32),
        'c_sample': nrm(ks[3], (DEC_BATCH, D_MODEL), jnp.float32),
        'cache_k': nrm(ks[4], (DEPTH, n_phys, PAGE_SIZE, N_HEADS, HEAD_DIM), jnp.float32),
        'cache_v': nrm(ks[5], (DEPTH, n_phys, PAGE_SIZE, N_HEADS, HEAD_DIM), jnp.float32),
        'state_conv': nrm(ks[6], (DEPTH, DEC_BATCH, CONV_K - 1, CONV_WIDTH), jnp.float32),
        'state_ffn': nrm(ks[7], (DEPTH, DEC_BATCH, FFN_K - 1, 2 * D_FF), jnp.float32),
        'page_table': page_table,
        'rel_bias': 0.5 * nrm(ks[9], (N_BUCKETS, N_HEADS), jnp.float32),
        'ada_w': nrm(ks[10], (DEPTH, D_MODEL, 6 * D_MODEL), jnp.float32) * (0.5 * D_MODEL ** -0.5),
        'ada_b': 0.01 * nrm(ks[11], (DEPTH, 6 * D_MODEL), jnp.float32),
        'norm_mix_g': 1.0 + 0.02 * nrm(ks[12], (DEPTH, D_MODEL), jnp.float32),
        'norm_ffn_g': 1.0 + 0.02 * nrm(ks[13], (DEPTH, D_MODEL), jnp.float32),
        'final_norm_g': 1.0 + 0.02 * nrm(ks[14], (D_MODEL,), jnp.float32),
        'w_in': nrm(ks[15], (DEPTH, D_MODEL, N_IN), jnp.float32) * D_MODEL ** -0.5,
        'conv_w': nrm(ks[16], (DEPTH, CONV_K, CONV_WIDTH), jnp.float32) * CONV_K ** -0.5,
        'w_att_out': nrm(ks[17], (DEPTH, ATT_WIDTH, D_MODEL), jnp.float32) * ATT_WIDTH ** -0.5,
        'w_conv_out': nrm(ks[18], (DEPTH, CONV_WIDTH, D_MODEL), jnp.float32) * CONV_WIDTH ** -0.5,
        'w_o': nrm(ks[19], (DEPTH, D_MODEL, D_MODEL), jnp.float32) * D_MODEL ** -0.5,
        'w_up': nrm(ks[20], (DEPTH, D_MODEL, 2 * D_FF), jnp.float32) * D_MODEL ** -0.5,
        'ffn_conv_w': nrm(ks[21], (DEPTH, FFN_K, 2 * D_FF), jnp.float32) * FFN_K ** -0.5,
        'ffn_conv_b': 0.01 * nrm(ks[22], (DEPTH, 2 * D_FF), jnp.float32),
        'w_down': nrm(ks[23], (DEPTH, D_FF, D_MODEL), jnp.float32) * D_FF ** -0.5,
    }


def reference(x_prompt, x_sample, c_prompt, c_sample, cache_k, cache_v, state_conv, state_ffn,
              page_table, rel_bias, ada_w, ada_b, norm_mix_g, norm_ffn_g, final_norm_g, w_in,
              conv_w, w_att_out, w_conv_out, w_o, w_up, ffn_conv_w, ffn_conv_b, w_down):
    db, n_pages = page_table.shape
    past_len = n_pages * PAGE_SIZE
    xp, xs = x_prompt, x_sample
    bp = xp.shape[0]
    conv0 = jnp.zeros((bp, CONV_K - 1, CONV_WIDTH), xp.dtype)
    ffn0 = jnp.zeros((bp, FFN_K - 1, 2 * D_FF), xp.dtype)
    attend_p = functools.partial(moba_prompt, rel_bias=rel_bias)
    kp_l, vp_l, cp_l, fp_l, ks_l, vs_l, cs_l, fs_l = [], [], [], [], [], [], [], []
    for l in range(DEPTH):
        w = (ada_w[l], ada_b[l], norm_mix_g[l], norm_ffn_g[l], w_in[l], conv_w[l], w_att_out[l],
             w_conv_out[l], w_o[l], w_up[l], ffn_conv_w[l], ffn_conv_b[l], w_down[l])
        xp, kp, vp, cp, fp = trunk_layer(xp, c_prompt, attend_p, conv0, ffn0, *w)
        past_k = cache_k[l][page_table].reshape(db, past_len, N_HEADS, HEAD_DIM)
        past_v = cache_v[l][page_table].reshape(db, past_len, N_HEADS, HEAD_DIM)
        attend_s = functools.partial(moba_sample, past_k=past_k, past_v=past_v, rel_bias=rel_bias)
        xs, k_s, v_s, c_s, f_s = trunk_layer(xs, c_sample, attend_s, state_conv[l], state_ffn[l], *w)
        kp_l.append(kp); vp_l.append(vp); cp_l.append(cp); fp_l.append(fp)
        ks_l.append(k_s); vs_l.append(v_s); cs_l.append(c_s); fs_l.append(f_s)
    y_prompt = rmsnorm(xp, final_norm_g)
    y_sample = rmsnorm(xs, final_norm_g)
    return (y_prompt, y_sample, jnp.stack(kp_l), jnp.stack(vp_l), jnp.stack(cp_l), jnp.stack(fp_l),
            jnp.stack(ks_l), jnp.stack(vs_l), jnp.stack(cs_l), jnp.stack(fs_l))
```

```python
import functools

import jax
import jax.numpy as jnp
from jax import lax
from jax.experimental import pallas as pl
from jax.experimental.pallas import tpu as pltpu

D_MODEL = 1024
N_HEADS = 8
HEAD_DIM = 64
ATT_WIDTH = N_HEADS * HEAD_DIM
CONV_WIDTH = 512
BLOCK = 256
TOP_K = 3
N_BUCKETS = 32
MAX_DIST = 1024
D_FF = 2816
PAGE_SIZE = 128
EPS = 1e-6
NEG = -1e30
REMOVED = -3e38
N_IN = 3 * ATT_WIDTH + 3 * CONV_WIDTH + 2 * D_MODEL
LANES = 128
FFN_CHUNK = 256
N_BIAS_TILES = 6
VMEM_LIMIT = 56 * 1024 * 1024

_BF = jnp.bfloat16
_F32 = jnp.float32


def _bucket_upper_bounds():
    max_exact = N_BUCKETS // 2
    n_log = N_BUCKETS - max_exact
    ratio = MAX_DIST // max_exact
    assert ratio * max_exact == MAX_DIST
    ups = [b + 1 for b in range(max_exact)]
    for k in range(1, n_log):
        target = (max_exact ** n_log) * (ratio ** k)
        n = max_exact
        while n ** n_log < target:
            n += 1
        ups.append(n)
    return ups


_BUCKET_UPPER = _bucket_upper_bounds()


def _dot(a, b):
    return jnp.dot(a, b, preferred_element_type=_F32)


def _dot_nt(a, b):
    return lax.dot_general(a, b, (((1,), (1,)), ((), ())), preferred_element_type=_F32)


def _sigmoid(x):
    return 1.0 / (1.0 + jnp.exp(-x))


def _silu(x):
    return x * _sigmoid(x)


def _rmsnorm(x, g):
    return x * lax.rsqrt(jnp.mean(x * x, axis=-1, keepdims=True) + EPS) * g


def _mod_chunk(mod_ref, idx, reps):
    v = mod_ref[0, :, idx * D_MODEL:(idx + 1) * D_MODEL]
    if reps > 1:
        v = jnp.concatenate([v] * reps, axis=0)
    return v


def _params(sem, vmem=VMEM_LIMIT):
    return pltpu.CompilerParams(dimension_semantics=sem, vmem_limit_bytes=vmem)


def _const_spec(shape):
    nd = len(shape)
    return pl.BlockSpec(shape, lambda *_: (0,) * nd, pipeline_mode=pl.Buffered(1))


def _mod_kernel(c_ref, w_ref, b_ref, o_ref):
    a = _silu(c_ref[...]).astype(_BF)
    o_ref[0] = _dot(a, w_ref[0].astype(_BF)) + b_ref[0]


def _modulation(c_all, ada_w, ada_b):
    depth = ada_w.shape[0]
    rows = c_all.shape[0]
    n_chunks = ada_w.shape[2] // D_MODEL
    return pl.pallas_call(
        _mod_kernel,
        grid=(depth, n_chunks),
        in_specs=[
            pl.BlockSpec((rows, D_MODEL), lambda l, n: (0, 0)),
            pl.BlockSpec((1, D_MODEL, D_MODEL), lambda l, n: (l, 0, n)),
            pl.BlockSpec((1, 1, D_MODEL), lambda l, n: (l, 0, n)),
        ],
        out_specs=pl.BlockSpec((1, rows, D_MODEL), lambda l, n: (l, 0, n)),
        out_shape=jax.ShapeDtypeStruct((depth, rows, ada_w.shape[2]), _F32),
        compiler_params=_params(("arbitrary", "arbitrary")),
        name="modulation",
    )(c_all, ada_w, ada_b.reshape(depth, 1, -1))


def _bias_kernel(rb_ref, bp_ref, bs_ref, *, past_len):
    h = pl.program_id(0)

    def table(dist):
        n = jnp.maximum(dist, 0)
        v = jnp.full(dist.shape, rb_ref[N_BUCKETS - 1, h], _F32)
        for b in range(N_BUCKETS - 2, -1, -1):
            v = jnp.where(n < _BUCKET_UPPER[b], rb_ref[b, h], v)
        return jnp.where(dist >= 0, v, NEG)

    kk = lax.broadcasted_iota(jnp.int32, (BLOCK, BLOCK), 0)
    qq = lax.broadcasted_iota(jnp.int32, (BLOCK, BLOCK), 1)
    for d in range(N_BIAS_TILES):
        bp_ref[0, d] = table(d * BLOCK + qq - kk)
    t = lax.broadcasted_iota(jnp.int32, bs_ref.shape[1:], 0)
    p = lax.broadcasted_iota(jnp.int32, bs_ref.shape[1:], 1)
    bs_ref[0] = table(past_len + t - p)


def _bias_tables(rel_bias, past_len):
    s_cols = past_len + PAGE_SIZE
    return pl.pallas_call(
        functools.partial(_bias_kernel, past_len=past_len),
        grid=(N_HEADS,),
        in_specs=[pl.BlockSpec(memory_space=pltpu.SMEM)],
        out_specs=[
            pl.BlockSpec((1, N_BIAS_TILES, BLOCK, BLOCK), lambda h: (h, 0, 0, 0)),
            pl.BlockSpec((1, 8, s_cols), lambda h: (h, 0, 0)),
        ],
        out_shape=[
            jax.ShapeDtypeStruct((N_HEADS, N_BIAS_TILES, BLOCK, BLOCK), _F32),
            jax.ShapeDtypeStruct((N_HEADS, 8, s_cols), _F32),
        ],
        compiler_params=_params(("arbitrary",)),
        name="bias_tables",
    )(rel_bias)


def _project(x, g, sh, sc, w_ref):
    hb = (_rmsnorm(x, g) * (1.0 + sc) + sh).astype(_BF)
    bounds = (0, ATT_WIDTH, 2 * ATT_WIDTH, 3 * ATT_WIDTH, 3 * ATT_WIDTH + CONV_WIDTH,
              3 * ATT_WIDTH + 2 * CONV_WIDTH, 3 * ATT_WIDTH + 3 * CONV_WIDTH,
              3 * ATT_WIDTH + 3 * CONV_WIDTH + D_MODEL, N_IN)
    return [_dot(hb, w_ref[:, a:b]) for a, b in zip(bounds[:-1], bounds[1:])]


def _inproj_prompt_kernel(x_ref, mod_ref, g_ref, w_ref, cw_ref,
                          k_ref, v_ref, kbf_ref, qtz_ref, vt_ref, kmean_ref, cb_ref, ga_ref, gb_ref,
                          cstate_ref, ext_ref):
    tr = x_ref.shape[1]
    t = pl.program_id(1)

    @pl.when(t == 0)
    def _():
        ext_ref[0:8, :] = jnp.zeros((8, CONV_WIDTH), _F32)

    q, k, v, u, bg, cg, ga, gb = _project(
        x_ref[0], g_ref[...], _mod_chunk(mod_ref, 0, 1), _mod_chunk(mod_ref, 1, 1), w_ref)

    qt = jnp.transpose(q * (HEAD_DIM ** -0.5))
    zeros = jnp.zeros((HEAD_DIM, tr), _F32)
    for h in range(N_HEADS):
        qh = qt[h * HEAD_DIM:(h + 1) * HEAD_DIM]
        pair = [qh, zeros] if h % 2 == 0 else [zeros, qh]
        qtz_ref[0, h] = jnp.concatenate(pair, axis=0).astype(_BF)

    k_ref[0] = k
    kbf_ref[0] = k.astype(_BF)
    kmean_ref[0, 0] = jnp.mean(k, axis=0, keepdims=True)
    v_ref[0] = v
    vt = jnp.transpose(v)
    for h in range(N_HEADS):
        vt_ref[0, h, 0] = vt[h * HEAD_DIM:(h + 1) * HEAD_DIM].astype(_BF)

    ext_ref[8:8 + tr, :] = cg * u
    cw = cw_ref[...]
    conv = (cw[0:1] * ext_ref[6:6 + tr, :] + cw[1:2] * ext_ref[7:7 + tr, :]
            + cw[2:3] * ext_ref[8:8 + tr, :])
    cb_ref[0] = (bg * conv).astype(_BF)
    cstate_ref[0] = ext_ref[tr + 6:tr + 8, :]
    ext_ref[0:8, :] = ext_ref[tr:tr + 8, :]

    ga_ref[0] = _sigmoid(ga)
    gb_ref[0] = _sigmoid(gb)


def _inproj_prompt(x, mod, g, w_bf, cw):
    bsz, seq, _ = x.shape
    tr = BLOCK
    nt = seq // tr
    row = lambda w: pl.BlockSpec((1, tr, w), lambda b, t: (b, t, 0))
    return pl.pallas_call(
        _inproj_prompt_kernel,
        grid=(bsz, nt),
        in_specs=[
            row(D_MODEL),
            pl.BlockSpec((1, 1, mod.shape[2]), lambda b, t: (b, 0, 0)),
            _const_spec((1, D_MODEL)),
            _const_spec((D_MODEL, N_IN)),
            _const_spec((3, CONV_WIDTH)),
        ],
        out_specs=[
            row(ATT_WIDTH), row(ATT_WIDTH), row(ATT_WIDTH),
            pl.BlockSpec((1, N_HEADS, 2 * HEAD_DIM, tr), lambda b, t: (b, 0, 0, t)),
            pl.BlockSpec((1, N_HEADS, 1, HEAD_DIM, tr), lambda b, t: (b, 0, t, 0, 0)),
            pl.BlockSpec((1, 1, 1, ATT_WIDTH), lambda b, t: (b, t, 0, 0)),
            row(CONV_WIDTH), row(D_MODEL), row(D_MODEL),
            pl.BlockSpec((1, 2, CONV_WIDTH), lambda b, t: (b, 0, 0)),
        ],
        out_shape=[
            jax.ShapeDtypeStruct((bsz, seq, ATT_WIDTH), _F32),
            jax.ShapeDtypeStruct((bsz, seq, ATT_WIDTH), _F32),
            jax.ShapeDtypeStruct((bsz, seq, ATT_WIDTH), _BF),
            jax.ShapeDtypeStruct((bsz, N_HEADS, 2 * HEAD_DIM, seq), _BF),
            jax.ShapeDtypeStruct((bsz, N_HEADS, nt, HEAD_DIM, tr), _BF),
            jax.ShapeDtypeStruct((bsz, nt, 1, ATT_WIDTH), _F32),
            jax.ShapeDtypeStruct((bsz, seq, CONV_WIDTH), _BF),
            jax.ShapeDtypeStruct((bsz, seq, D_MODEL), _F32),
            jax.ShapeDtypeStruct((bsz, seq, D_MODEL), _F32),
            jax.ShapeDtypeStruct((bsz, 2, CONV_WIDTH), _F32),
        ],
        scratch_shapes=[pltpu.VMEM((tr + 8, CONV_WIDTH), _F32)],
        compiler_params=_params(("arbitrary", "arbitrary")),
        name="inproj_prompt",
    )(x, mod, g, w_bf, cw)


def _inproj_sample_kernel(x_ref, mod_ref, g_ref, w_ref, cw_ref, prev_ref,
                          q_ref, k_ref, v_ref, cb_ref, ga_ref, gb_ref, cstate_ref, *, reps):
    rows = x_ref.shape[0]
    stride = rows // reps
    q, k, v, u, bg, cg, ga, gb = _project(
        x_ref[...], g_ref[...], _mod_chunk(mod_ref, 0, reps), _mod_chunk(mod_ref, 1, reps), w_ref)
    q_ref[...] = q * (HEAD_DIM ** -0.5)
    k_ref[...] = k
    v_ref[...] = v
    cu = cg * u
    ext = jnp.concatenate([prev_ref[...], cu], axis=0)
    cw = cw_ref[...]
    conv = (cw[0:1] * ext[0:rows] + cw[1:2] * ext[stride:stride + rows]
            + cw[2:3] * ext[2 * stride:2 * stride + rows])
    cb_ref[...] = (bg * conv).astype(_BF)
    cstate_ref[...] = ext[rows:rows + 2 * stride]
    ga_ref[...] = _sigmoid(ga)
    gb_ref[...] = _sigmoid(gb)


def _inproj_sample(x, mod, g, w_bf, cw, prev, reps):
    rows = x.shape[0]
    full = lambda a: pl.BlockSpec(a.shape, lambda i: (0,) * a.ndim)
    shp = lambda w, dt=_F32: jax.ShapeDtypeStruct((rows, w), dt)
    outs = [shp(ATT_WIDTH), shp(ATT_WIDTH), shp(ATT_WIDTH), shp(CONV_WIDTH, _BF),
            shp(D_MODEL), shp(D_MODEL), jax.ShapeDtypeStruct(prev.shape, _F32)]
    return pl.pallas_call(
        functools.partial(_inproj_sample_kernel, reps=reps),
        grid=(1,),
        in_specs=[full(x), full(mod), full(g), full(w_bf), full(cw), full(prev)],
        out_specs=[pl.BlockSpec(o.shape, lambda i: (0, 0)) for o in outs],
        out_shape=outs,
        compiler_params=_params(("arbitrary",)),
        name="inproj_sample",
    )(x, mod, g, w_bf, cw, prev)


def _top_k_rows(gate, limit):
    n = gate.shape[0]
    row = lax.broadcasted_iota(jnp.int32, gate.shape, 0).astype(_F32)
    past = row < limit
    gate = jnp.where(past, gate, NEG)
    mask = jnp.full(gate.shape, NEG, _F32)
    for _ in range(min(TOP_K, n)):
        mx = jnp.max(gate, axis=0, keepdims=True)
        first = jnp.min(jnp.where(gate == mx, row, float(n)), axis=0, keepdims=True)
        chosen = row == first
        mask = jnp.where(chosen, jnp.where(past, 0.0, NEG), mask)
        gate = jnp.where(chosen, REMOVED, gate)
    return mask


def _attn_prompt_kernel(qtz_ref, k_ref, vt_ref, kmean_ref, bias_ref, o_ref, sel_ref):
    i = pl.program_id(2)
    tq = qtz_ref.shape[3]
    qz = qtz_ref[0, 0]
    gate = _dot(kmean_ref[0].astype(_BF), qz)
    sel_ref[...] = _top_k_rows(gate, i.astype(_F32))

    def step(j, bias, carry, use_sel):
        m, l, acc = carry
        s = _dot(k_ref[0, j], qz) + bias
        if use_sel:
            s = s + sel_ref[pl.ds(j, 1), :]
        m_new = jnp.maximum(m, jnp.max(s, axis=0, keepdims=True))
        alpha = jnp.exp(m - m_new)
        p = jnp.exp(s - m_new)
        l = alpha * l + jnp.sum(p, axis=0, keepdims=True)
        acc = alpha * acc + _dot(vt_ref[0, 0, j], p.astype(_BF))
        return m_new, l, acc

    init = (jnp.full((1, tq), NEG, _F32), jnp.zeros((1, tq), _F32), jnp.zeros((HEAD_DIM, tq), _F32))
    carry = lax.fori_loop(
        0, i, lambda j, c: step(j, bias_ref[0, jnp.minimum(i - j, N_BIAS_TILES - 1)], c, True), init)
    _, l, acc = step(i, bias_ref[0, 0], carry, False)
    o_ref[0] = (acc / l).astype(_BF)


def _attn_prompt(qtz, kbf, vt, kmean, bias_p):
    bsz, _, _, seq = qtz.shape
    nb = seq // BLOCK
    kb4 = kbf.reshape(bsz, nb, BLOCK, ATT_WIDTH)
    km3 = kmean.reshape(bsz, nb, ATT_WIDTH)
    return pl.pallas_call(
        _attn_prompt_kernel,
        grid=(bsz, N_HEADS, nb),
        in_specs=[
            pl.BlockSpec((1, 1, 2 * HEAD_DIM, BLOCK), lambda b, h, i: (b, h, 0, i)),
            pl.BlockSpec((1, nb, BLOCK, LANES), lambda b, h, i: (b, 0, 0, h // 2)),
            pl.BlockSpec((1, 1, nb, HEAD_DIM, BLOCK), lambda b, h, i: (b, h, 0, 0, 0)),
            pl.BlockSpec((1, nb, LANES), lambda b, h, i: (b, 0, h // 2)),
            pl.BlockSpec((1, N_BIAS_TILES, BLOCK, BLOCK), lambda b, h, i: (h, 0, 0, 0)),
        ],
        out_specs=pl.BlockSpec((1, HEAD_DIM, BLOCK), lambda b, h, i: (b, h, i)),
        out_shape=jax.ShapeDtypeStruct((bsz, ATT_WIDTH, seq), _BF),
        scratch_shapes=[pltpu.VMEM((nb, BLOCK), _F32)],
        compiler_params=_params(("arbitrary", "arbitrary", "arbitrary")),
        name="attn_prompt",
    )(qtz, kb4, vt, km3, bias_p)


def _attn_sample_kernel(pt_ref, q_ref, kn_ref, vn_ref, bias_ref, *refs, n_pages, n_q):
    del pt_ref
    k_refs, v_refs, o_ref = refs[:n_pages], refs[n_pages:2 * n_pages], refs[2 * n_pages]
    rows = n_q * N_HEADS
    pages_per_block = BLOCK // PAGE_SIZE
    n_blocks = n_pages // pages_per_block

    q = q_ref[0]
    rep = jnp.concatenate([jnp.broadcast_to(q[t:t + 1], (N_HEADS, ATT_WIDTH)) for t in range(n_q)], axis=0)
    r_head = lax.broadcasted_iota(jnp.int32, (rows, ATT_WIDTH), 0) % N_HEADS
    c_head = lax.broadcasted_iota(jnp.int32, (rows, ATT_WIDTH), 1) // HEAD_DIM
    head_lanes = r_head == c_head
    q_rows = jnp.where(head_lanes, rep, 0.0).astype(_BF)

    k_pages = [r[0, 0] for r in k_refs]
    sums = [jnp.sum(kp, axis=0, keepdims=True) for kp in k_pages]
    kmean = jnp.concatenate(
        [sum(sums[n * pages_per_block:(n + 1) * pages_per_block]) for n in range(n_blocks)],
        axis=0) * (1.0 / BLOCK)
    gate = _dot_nt(q_rows, kmean.astype(_BF))

    col = lax.broadcasted_iota(jnp.int32, gate.shape, 1).astype(_F32)
    sel = jnp.full(gate.shape, NEG, _F32)
    for _ in range(min(TOP_K, n_blocks)):
        mx = jnp.max(gate, axis=1, keepdims=True)
        first = jnp.min(jnp.where(gate == mx, col, float(n_blocks)), axis=1, keepdims=True)
        chosen = col == first
        sel = jnp.where(chosen, 0.0, sel)
        gate = jnp.where(chosen, REMOVED, gate)

    bias = bias_ref[...]
    logits = []
    for p in range(n_pages):
        s = _dot_nt(q_rows, k_pages[p].astype(_BF)) + bias[:, p * PAGE_SIZE:(p + 1) * PAGE_SIZE]
        n = p // pages_per_block
        logits.append(s + jnp.broadcast_to(sel[:, n:n + 1], s.shape))
    pad = jnp.zeros((PAGE_SIZE - kn_ref.shape[1], ATT_WIDTH), _F32)
    k_own = jnp.concatenate([kn_ref[0], pad], axis=0).astype(_BF)
    v_own = jnp.concatenate([vn_ref[0], pad], axis=0).astype(_BF)
    logits.append(_dot_nt(q_rows, k_own) + bias[:, n_pages * PAGE_SIZE:])

    m = functools.reduce(jnp.maximum, logits)
    m = jnp.max(m, axis=1, keepdims=True)
    l = jnp.zeros((rows, 1), _F32)
    acc = jnp.zeros((rows, ATT_WIDTH), _F32)
    for p in range(n_pages + 1):
        pr = jnp.exp(logits[p] - m)
        l = l + jnp.sum(pr, axis=1, keepdims=True)
        vp = v_own if p == n_pages else v_refs[p][0, 0].astype(_BF)
        acc = acc + _dot(pr.astype(_BF), vp)
    out = jnp.where(head_lanes, acc / l, 0.0)
    o_ref[0] = jnp.sum(out.reshape(n_q, N_HEADS, ATT_WIDTH), axis=1)


def _attn_sample(page_table, q, k_new, v_new, bias_s, cache_k4, cache_v4, layer):
    n_seq, n_pages = page_table.shape
    n_q = q.shape[1]
    page_spec = lambda p: pl.BlockSpec(
        (1, 1, PAGE_SIZE, ATT_WIDTH), lambda b, pt, p=p: (layer, pt[b, p], 0, 0))
    per_seq = lambda a: pl.BlockSpec((1,) + a.shape[1:], lambda b, pt: (b, 0, 0))
    grid_spec = pltpu.PrefetchScalarGridSpec(
        num_scalar_prefetch=1,
        grid=(n_seq,),
        in_specs=[per_seq(q), per_seq(k_new), per_seq(v_new),
                  pl.BlockSpec(bias_s.shape, lambda b, pt: (0, 0))]
                 + [page_spec(p) for p in range(n_pages)] * 2,
        out_specs=pl.BlockSpec((1, n_q, ATT_WIDTH), lambda b, pt: (b, 0, 0)),
    )
    return pl.pallas_call(
        functools.partial(_attn_sample_kernel, n_pages=n_pages, n_q=n_q),
        grid_spec=grid_spec,
        out_shape=jax.ShapeDtypeStruct((n_seq, n_q, ATT_WIDTH), _F32),
        compiler_params=_params(("arbitrary",)),
        name="attn_sample",
    )(page_table, q, k_new, v_new, bias_s, *([cache_k4] * n_pages), *([cache_v4] * n_pages))


def _post_kernel(x_ref, att_ref, cb_ref, ga_ref, gb_ref, mod_ref, g_ref, wa_ref, wc_ref, wo_ref,
                 x1_ref, h2_ref, *, reps, att_transposed):
    if att_transposed:
        att = jnp.transpose(att_ref[0].astype(_F32)).astype(_BF)
    else:
        att = att_ref[0].astype(_BF)
    y_a = _dot(att, wa_ref[...])
    y_b = _dot(cb_ref[0], wc_ref[...])
    merged = ga_ref[0] * y_a + gb_ref[0] * y_b
    x1 = x_ref[0] + _mod_chunk(mod_ref, 2, reps) * _dot(merged.astype(_BF), wo_ref[...])
    x1_ref[0] = x1
    h2 = _rmsnorm(x1, g_ref[...]) * (1.0 + _mod_chunk(mod_ref, 4, reps)) + _mod_chunk(mod_ref, 3, reps)
    h2_ref[0] = h2.astype(_BF)


def _post(x, att, cb, ga, gb, mod, g, wa, wc, wo, *, tr, att_transposed):
    bsz, seq, _ = x.shape
    nt = seq // tr
    reps = 1 if mod.shape[1] == 1 else tr // mod.shape[1]
    row = lambda w: pl.BlockSpec((1, tr, w), lambda b, t: (b, t, 0))
    att_spec = (pl.BlockSpec((1, ATT_WIDTH, tr), lambda b, t: (b, 0, t)) if att_transposed
                else row(ATT_WIDTH))
    return pl.pallas_call(
        functools.partial(_post_kernel, reps=reps, att_transposed=att_transposed),
        grid=(bsz, nt),
        in_specs=[
            row(D_MODEL), att_spec, row(CONV_WIDTH), row(D_MODEL), row(D_MODEL),
            pl.BlockSpec((1,) + mod.shape[1:], lambda b, t: (b, 0, 0)),
            _const_spec((1, D_MODEL)),
            _const_spec(wa.shape), _const_spec(wc.shape), _const_spec(wo.shape),
        ],
        out_specs=[row(D_MODEL), row(D_MODEL)],
        out_shape=[jax.ShapeDtypeStruct(x.shape, _F32), jax.ShapeDtypeStruct(x.shape, _BF)],
        compiler_params=_params(("arbitrary", "arbitrary")),
        name="post_attn",
    )(x, att, cb, ga, gb, mod, g, wa, wc, wo)


def _ffn_kernel(*refs, reps, stride, final):
    if stride == 1:
        x1_ref, h2_ref, mod_ref, wu_ref, cw_ref, cbias_ref, wd_ref = refs[:7]
        rest = refs[7:]
        prev_ref = None
    else:
        x1_ref, h2_ref, mod_ref, wu_ref, cw_ref, cbias_ref, wd_ref, prev_ref = refs[:8]
        rest = refs[8:]
    if final:
        fg_ref, rest = rest[0], rest[1:]
    out_ref, state_ref = rest[0], rest[1]
    ext_ref = rest[2] if stride == 1 else None

    rows = x1_ref.shape[1]
    hb = h2_ref[0]
    cw = cw_ref[...]
    cbias = cbias_ref[...]

    if stride == 1:
        @pl.when(pl.program_id(1) == 0)
        def _():
            ext_ref[0:8, :] = jnp.zeros((8, 2 * D_FF), _F32)

    def conv_cols(sl):
        up = _dot(hb, wu_ref[:, sl])
        if stride == 1:
            ext_ref[8:8 + rows, sl] = up
            e0, e1, e2 = ext_ref[6:6 + rows, sl], ext_ref[7:7 + rows, sl], up
        else:
            ext = jnp.concatenate([prev_ref[:, sl], up], axis=0)
            e0, e1, e2 = ext[0:rows], ext[stride:stride + rows], up
            state_ref[:, sl] = ext[rows:rows + 2 * stride]
        return cw[0:1, sl] * e0 + cw[1:2, sl] * e1 + cw[2:3, sl] * e2 + cbias[:, sl]

    acc = jnp.zeros((rows, D_MODEL), _F32)
    for c in range(D_FF // FFN_CHUNK):
        a = conv_cols(slice(c * FFN_CHUNK, (c + 1) * FFN_CHUNK))
        b = conv_cols(slice(D_FF + c * FFN_CHUNK, D_FF + (c + 1) * FFN_CHUNK))
        acc = acc + _dot((_silu(a) * b).astype(_BF), wd_ref[c * FFN_CHUNK:(c + 1) * FFN_CHUNK, :])

    if stride == 1:
        state_ref[0] = ext_ref[rows + 6:rows + 8, :]
        ext_ref[0:8, :] = ext_ref[rows:rows + 8, :]

    x2 = x1_ref[0] + _mod_chunk(mod_ref, 5, reps) * acc
    out_ref[0] = _rmsnorm(x2, fg_ref[...]) if final else x2


def _ffn(x1, h2, mod, wu, cw, cbias, wd, prev, final_g, *, tr, stride):
    bsz, seq, _ = x1.shape
    nt = seq // tr
    reps = 1 if mod.shape[1] == 1 else tr // mod.shape[1]
    final = final_g is not None
    row = lambda w: pl.BlockSpec((1, tr, w), lambda b, t: (b, t, 0))
    in_specs = [row(D_MODEL), row(D_MODEL),
                pl.BlockSpec((1,) + mod.shape[1:], lambda b, t: (b, 0, 0)),
                _const_spec(wu.shape), _const_spec(cw.shape), _const_spec(cbias.shape), _const_spec(wd.shape)]
    args = [x1, h2, mod, wu, cw, cbias, wd]
    if stride == 1:
        state_shape = jax.ShapeDtypeStruct((bsz, 2, 2 * D_FF), _F32)
        state_spec = pl.BlockSpec((1, 2, 2 * D_FF), lambda b, t: (b, 0, 0))
        scratch = [pltpu.VMEM((tr + 8, 2 * D_FF), _F32)]
    else:
        assert bsz == 1 and nt == 1 and tr % stride == 0
        in_specs.append(_const_spec(prev.shape))
        args.append(prev)
        state_shape = jax.ShapeDtypeStruct(prev.shape, _F32)
        state_spec = pl.BlockSpec(prev.shape, lambda b, t: (0, 0))
        scratch = []
    if final:
        in_specs.append(_const_spec(final_g.shape))
        args.append(final_g)
    return pl.pallas_call(
        functools.partial(_ffn_kernel, reps=reps, stride=stride, final=final),
        grid=(bsz, nt),
        in_specs=in_specs,
        out_specs=[row(D_MODEL), state_spec],
        out_shape=[jax.ShapeDtypeStruct(x1.shape, _F32), state_shape],
        scratch_shapes=scratch,
        compiler_params=_params(("arbitrary", "arbitrary")),
        name="conv_ffn",
    )(*args)


def kernel(x_prompt, x_sample, c_prompt, c_sample, cache_k, cache_v, state_conv, state_ffn, page_table,
           rel_bias, ada_w, ada_b, norm_mix_g, norm_ffn_g, final_norm_g, w_in, conv_w, w_att_out,
           w_conv_out, w_o, w_up, ffn_conv_w, ffn_conv_b, w_down):
    depth = w_in.shape[0]
    bsz, seq, _ = x_prompt.shape
    n_seq, n_q, _ = x_sample.shape
    n_pages = page_table.shape[1]
    past_len = n_pages * PAGE_SIZE
    assert seq % BLOCK == 0 and past_len % BLOCK == 0 and n_q <= 8 and bsz <= 8
    assert cache_k.shape[2] == PAGE_SIZE

    pad_rows = 8
    c_all = jnp.concatenate([c_prompt, jnp.zeros((pad_rows - bsz, D_MODEL), _F32), c_sample], axis=0)
    mod_all = _modulation(c_all, ada_w, ada_b)
    mod_p = mod_all[:, :bsz].reshape(depth, bsz, 1, -1)
    mod_s = mod_all[:, pad_rows:].reshape(depth, 1, n_seq, -1)

    bias_p, bias_s8 = _bias_tables(rel_bias, past_len)
    bias_s = bias_s8[:, :n_q].transpose(1, 0, 2).reshape(n_q * N_HEADS, -1)

    cache_k4 = cache_k.reshape(cache_k.shape[:3] + (ATT_WIDTH,))
    cache_v4 = cache_v.reshape(cache_v.shape[:3] + (ATT_WIDTH,))
    time_major = lambda a: a.transpose(1, 0, 2).reshape(-1, a.shape[2])
    seq_major = lambda a, steps: a.reshape(steps, n_seq, -1).transpose(1, 0, 2)
    rows_s = n_q * n_seq

    xp = x_prompt
    xs = time_major(x_sample)[None]
    fin = final_norm_g.reshape(1, D_MODEL)
    outs = [[] for _ in range(8)]
    for l in range(depth):
        g_mix = norm_mix_g[l].reshape(1, D_MODEL)
        g_ffn = norm_ffn_g[l].reshape(1, D_MODEL)
        w_in_bf = w_in[l].astype(_BF)
        wa, wc, wo = w_att_out[l].astype(_BF), w_conv_out[l].astype(_BF), w_o[l].astype(_BF)
        wu, wd = w_up[l].astype(_BF), w_down[l].astype(_BF)
        cbias = ffn_conv_b[l].reshape(1, -1)
        final_g = fin if l == depth - 1 else None

        k, v, kbf, qtz, vt, kmean, cb, ga, gb, cstate = _inproj_prompt(xp, mod_p[l], g_mix, w_in_bf, conv_w[l])
        att_t = _attn_prompt(qtz, kbf, vt, kmean, bias_p)
        x1, h2 = _post(xp, att_t, cb, ga, gb, mod_p[l], g_ffn, wa, wc, wo, tr=BLOCK, att_transposed=True)
        xp, fstate = _ffn(x1, h2, mod_p[l], wu, ffn_conv_w[l], cbias, wd, None, final_g, tr=BLOCK, stride=1)
        outs[0].append(k.reshape(bsz, seq, N_HEADS, HEAD_DIM))
        outs[1].append(v.reshape(bsz, seq, N_HEADS, HEAD_DIM))
        outs[2].append(cstate)
        outs[3].append(fstate)

        q_s, k_s, v_s, cb_s, ga_s, gb_s, cstate_s = _inproj_sample(
            xs[0], mod_s[l], g_mix, w_in_bf, conv_w[l], time_major(state_conv[l]), n_q)
        k_sm, v_sm = seq_major(k_s, n_q), seq_major(v_s, n_q)
        pad = ((0, 0), (0, 8 - n_q), (0, 0))
        att_s = _attn_sample(page_table, seq_major(q_s, n_q), jnp.pad(k_sm, pad), jnp.pad(v_sm, pad),
                             bias_s, cache_k4, cache_v4, l)
        x1s, h2s = _post(xs, time_major(att_s)[None], cb_s[None], ga_s[None], gb_s[None], mod_s[l], g_ffn,
                         wa, wc, wo, tr=rows_s, att_transposed=False)
        xs, fstate_s = _ffn(x1s, h2s, mod_s[l], wu, ffn_conv_w[l], cbias, wd, time_major(state_ffn[l]),
                            final_g, tr=rows_s, stride=n_seq)
        outs[4].append(k_sm.reshape(n_seq, n_q, N_HEADS, HEAD_DIM))
        outs[5].append(v_sm.reshape(n_seq, n_q, N_HEADS, HEAD_DIM))
        outs[6].append(seq_major(cstate_s, 2))
        outs[7].append(seq_major(fstate_s, 2))

    y_sample = seq_major(xs[0], n_q)
    return (xp, y_sample) + tuple(jnp.stack(o) for o in outs)
```

```python
import functools

import jax
import jax.numpy as jnp
from jax import lax
from jax.experimental import pallas as pl
from jax.experimental.pallas import tpu as pltpu

D_MODEL = 1024
N_HEADS = 8
HEAD_DIM = 64
ATT_WIDTH = N_HEADS * HEAD_DIM
CONV_WIDTH = 512
BLOCK = 256
TOP_K = 3
N_BUCKETS = 32
MAX_DIST = 1024
D_FF = 2816
PAGE_SIZE = 128
EPS = 1e-6
NEG = -1e30
REMOVED = -3e38
N_IN = 3 * ATT_WIDTH + 3 * CONV_WIDTH + 2 * D_MODEL
LANES = 128
FFN_CHUNK = 256
N_BIAS_TILES = 6
VMEM_LIMIT = 56 * 1024 * 1024

_BF = jnp.bfloat16
_F32 = jnp.float32


def _bucket_upper_bounds():
    max_exact = N_BUCKETS // 2
    n_log = N_BUCKETS - max_exact
    ratio = MAX_DIST // max_exact
    assert ratio * max_exact == MAX_DIST
    ups = [b + 1 for b in range(max_exact)]
    for k in range(1, n_log):
        target = (max_exact ** n_log) * (ratio ** k)
        n = max_exact
        while n ** n_log < target:
            n += 1
        ups.append(n)
    return ups


_BUCKET_UPPER = _bucket_upper_bounds()


def _dot(a, b):
    return jnp.dot(a, b, preferred_element_type=_F32)


def _dot_nt(a, b):
    return lax.dot_general(a, b, (((1,), (1,)), ((), ())), preferred_element_type=_F32)


def _sigmoid(x):
    return 1.0 / (1.0 + jnp.exp(-x))


def _silu(x):
    return x * _sigmoid(x)


def _rmsnorm(x, g):
    return x * lax.rsqrt(jnp.mean(x * x, axis=-1, keepdims=True) + EPS) * g


def _mod_chunk(mod_ref, idx, reps):
    v = mod_ref[0, :, idx * D_MODEL:(idx + 1) * D_MODEL]
    if reps > 1:
        v = jnp.concatenate([v] * reps, axis=0)
    return v


def _params(sem, vmem=VMEM_LIMIT):
    return pltpu.CompilerParams(dimension_semantics=sem, vmem_limit_bytes=vmem)


def _const_spec(shape):
    nd = len(shape)
    return pl.BlockSpec(shape, lambda *_: (0,) * nd, pipeline_mode=pl.Buffered(1))


def _mod_kernel(c_ref, w_ref, b_ref, o_ref):
    a = _silu(c_ref[...]).astype(_BF)
    o_ref[0] = _dot(a, w_ref[0].astype(_BF)) + b_ref[0]


def _modulation(c_all, ada_w, ada_b):
    depth = ada_w.shape[0]
    rows = c_all.shape[0]
    n_chunks = ada_w.shape[2] // D_MODEL
    return pl.pallas_call(
        _mod_kernel,
        grid=(depth, n_chunks),
        in_specs=[
            pl.BlockSpec((rows, D_MODEL), lambda l, n: (0, 0)),
            pl.BlockSpec((1, D_MODEL, D_MODEL), lambda l, n: (l, 0, n)),
            pl.BlockSpec((1, 1, D_MODEL), lambda l, n: (l, 0, n)),
        ],
        out_specs=pl.BlockSpec((1, rows, D_MODEL), lambda l, n: (l, 0, n)),
        out_shape=jax.ShapeDtypeStruct((depth, rows, ada_w.shape[2]), _F32),
        compiler_params=_params(("arbitrary", "arbitrary")),
        name="modulation",
    )(c_all, ada_w, ada_b.reshape(depth, 1, -1))


def _bias_kernel(rb_ref, bp_ref, bs_ref, *, past_len):
    h = pl.program_id(0)

    def table(dist):
        n = jnp.maximum(dist, 0)
        v = jnp.full(dist.shape, rb_ref[N_BUCKETS - 1, h], _F32)
        for b in range(N_BUCKETS - 2, -1, -1):
            v = jnp.where(n < _BUCKET_UPPER[b], rb_ref[b, h], v)
        return jnp.where(dist >= 0, v, NEG)

    kk = lax.broadcasted_iota(jnp.int32, (BLOCK, BLOCK), 0)
    qq = lax.broadcasted_iota(jnp.int32, (BLOCK, BLOCK), 1)
    for d in range(N_BIAS_TILES):
        bp_ref[0, d] = table(d * BLOCK + qq - kk)
    t = lax.broadcasted_iota(jnp.int32, bs_ref.shape[1:], 0)
    p = lax.broadcasted_iota(jnp.int32, bs_ref.shape[1:], 1)
    bs_ref[0] = table(past_len + t - p)


def _bias_tables(rel_bias, past_len):
    s_cols = past_len + PAGE_SIZE
    return pl.pallas_call(
        functools.partial(_bias_kernel, past_len=past_len),
        grid=(N_HEADS,),
        in_specs=[pl.BlockSpec(memory_space=pltpu.SMEM)],
        out_specs=[
            pl.BlockSpec((1, N_BIAS_TILES, BLOCK, BLOCK), lambda h: (h, 0, 0, 0)),
            pl.BlockSpec((1, 8, s_cols), lambda h: (h, 0, 0)),
        ],
        out_shape=[
            jax.ShapeDtypeStruct((N_HEADS, N_BIAS_TILES, BLOCK, BLOCK), _F32),
            jax.ShapeDtypeStruct((N_HEADS, 8, s_cols), _F32),
        ],
        compiler_params=_params(("arbitrary",)),
        name="bias_tables",
    )(rel_bias)


def _project(x, g, sh, sc, w_ref):
    hb = (_rmsnorm(x, g) * (1.0 + sc) + sh).astype(_BF)
    bounds = (0, ATT_WIDTH, 2 * ATT_WIDTH, 3 * ATT_WIDTH, 3 * ATT_WIDTH + CONV_WIDTH,
              3 * ATT_WIDTH + 2 * CONV_WIDTH, 3 * ATT_WIDTH + 3 * CONV_WIDTH,
              3 * ATT_WIDTH + 3 * CONV_WIDTH + D_MODEL, N_IN)
    return [_dot(hb, w_ref[:, a:b]) for a, b in zip(bounds[:-1], bounds[1:])]


def _inproj_prompt_kernel(x_ref, mod_ref, g_ref, w_ref, cw_ref,
                          k_ref, v_ref, kbf_ref, qtz_ref, vt_ref, kmean_ref, cb_ref, ga_ref, gb_ref,
                          cstate_ref, ext_ref):
    tr = x_ref.shape[1]
    t = pl.program_id(1)

    @pl.when(t == 0)
    def _():
        ext_ref[0:8, :] = jnp.zeros((8, CONV_WIDTH), _F32)

    q, k, v, u, bg, cg, ga, gb = _project(
        x_ref[0], g_ref[...], _mod_chunk(mod_ref, 0, 1), _mod_chunk(mod_ref, 1, 1), w_ref)

    qt = jnp.transpose(q * (HEAD_DIM ** -0.5))
    zeros = jnp.zeros((HEAD_DIM, tr), _F32)
    for h in range(N_HEADS):
        qh = qt[h * HEAD_DIM:(h + 1) * HEAD_DIM]
        pair = [qh, zeros] if h % 2 == 0 else [zeros, qh]
        qtz_ref[0, h] = jnp.concatenate(pair, axis=0).astype(_BF)

    k_ref[0] = k
    kbf_ref[0] = k.astype(_BF)
    kmean_ref[0, 0] = jnp.mean(k, axis=0, keepdims=True)
    v_ref[0] = v
    vt = jnp.transpose(v)
    for h in range(N_HEADS):
        vt_ref[0, h, 0] = vt[h * HEAD_DIM:(h + 1) * HEAD_DIM].astype(_BF)

    ext_ref[8:8 + tr, :] = cg * u
    cw = cw_ref[...]
    conv = (cw[0:1] * ext_ref[6:6 + tr, :] + cw[1:2] * ext_ref[7:7 + tr, :]
            + cw[2:3] * ext_ref[8:8 + tr, :])
    cb_ref[0] = (bg * conv).astype(_BF)
    cstate_ref[0] = ext_ref[tr + 6:tr + 8, :]
    ext_ref[0:8, :] = ext_ref[tr:tr + 8, :]

    ga_ref[0] = _sigmoid(ga)
    gb_ref[0] = _sigmoid(gb)


def _inproj_prompt(x, mod, g, w_bf, cw):
    bsz, seq, _ = x.shape
    tr = BLOCK
    nt = seq // tr
    row = lambda w: pl.BlockSpec((1, tr, w), lambda b, t: (b, t, 0))
    return pl.pallas_call(
        _inproj_prompt_kernel,
        grid=(bsz, nt),
        in_specs=[
            row(D_MODEL),
            pl.BlockSpec((1, 1, mod.shape[2]), lambda b, t: (b, 0, 0)),
            _const_spec((1, D_MODEL)),
            _const_spec((D_MODEL, N_IN)),
            _const_spec((3, CONV_WIDTH)),
        ],
        out_specs=[
            row(ATT_WIDTH), row(ATT_WIDTH), row(ATT_WIDTH),
            pl.BlockSpec((1, N_HEADS, 2 * HEAD_DIM, tr), lambda b, t: (b, 0, 0, t)),
            pl.BlockSpec((1, N_HEADS, 1, HEAD_DIM, tr), lambda b, t: (b, 0, t, 0, 0)),
            pl.BlockSpec((1, 1, 1, ATT_WIDTH), lambda b, t: (b, t, 0, 0)),
            row(CONV_WIDTH), row(D_MODEL), row(D_MODEL),
            pl.BlockSpec((1, 2, CONV_WIDTH), lambda b, t: (b, 0, 0)),
        ],
        out_shape=[
            jax.ShapeDtypeStruct((bsz, seq, ATT_WIDTH), _F32),
            jax.ShapeDtypeStruct((bsz, seq, ATT_WIDTH), _F32),
            jax.ShapeDtypeStruct((bsz, seq, ATT_WIDTH), _BF),
            jax.ShapeDtypeStruct((bsz, N_HEADS, 2 * HEAD_DIM, seq), _BF),
            jax.ShapeDtypeStruct((bsz, N_HEADS, nt, HEAD_DIM, tr), _BF),
            jax.ShapeDtypeStruct((bsz, nt, 1, ATT_WIDTH), _F32),
            jax.ShapeDtypeStruct((bsz, seq, CONV_WIDTH), _BF),
            jax.ShapeDtypeStruct((bsz, seq, D_MODEL), _F32),
            jax.ShapeDtypeStruct((bsz, seq, D_MODEL), _F32),
            jax.ShapeDtypeStruct((bsz, 2, CONV_WIDTH), _F32),
        ],
        scratch_shapes=[pltpu.VMEM((tr + 8, CONV_WIDTH), _F32)],
        compiler_params=_params(("arbitrary", "arbitrary")),
        name="inproj_prompt",
    )(x, mod, g, w_bf, cw)


def _inproj_sample_kernel(x_ref, mod_ref, g_ref, w_ref, cw_ref, prev_ref,
                          q_ref, k_ref, v_ref, cb_ref, ga_ref, gb_ref, cstate_ref, *, reps):
    rows = x_ref.shape[0]
    stride = rows // reps
    q, k, v, u, bg, cg, ga, gb = _project(
        x_ref[...], g_ref[...], _mod_chunk(mod_ref, 0, reps), _mod_chunk(mod_ref, 1, reps), w_ref)
    q_ref[...] = q * (HEAD_DIM ** -0.5)
    k_ref[...] = k
    v_ref[...] = v
    cu = cg * u
    ext = jnp.concatenate([prev_ref[...], cu], axis=0)
    cw = cw_ref[...]
    conv = (cw[0:1] * ext[0:rows] + cw[1:2] * ext[stride:stride + rows]
            + cw[2:3] * ext[2 * stride:2 * stride + rows])
    cb_ref[...] = (bg * conv).astype(_BF)
    cstate_ref[...] = ext[rows:rows + 2 * stride]
    ga_ref[...] = _sigmoid(ga)
    gb_ref[...] = _sigmoid(gb)


def _inproj_sample(x, mod, g, w_bf, cw, prev, reps):
    rows = x.shape[0]
    full = lambda a: pl.BlockSpec(a.shape, lambda i: (0,) * a.ndim)
    shp = lambda w, dt=_F32: jax.ShapeDtypeStruct((rows, w), dt)
    outs = [shp(ATT_WIDTH), shp(ATT_WIDTH), shp(ATT_WIDTH), shp(CONV_WIDTH, _BF),
            shp(D_MODEL), shp(D_MODEL), jax.ShapeDtypeStruct(prev.shape, _F32)]
    return pl.pallas_call(
        functools.partial(_inproj_sample_kernel, reps=reps),
        grid=(1,),
        in_specs=[full(x), full(mod), full(g), full(w_bf), full(cw), full(prev)],
        out_specs=[pl.BlockSpec(o.shape, lambda i: (0, 0)) for o in outs],
        out_shape=outs,
        compiler_params=_params(("arbitrary",)),
        name="inproj_sample",
    )(x, mod, g, w_bf, cw, prev)


def _top_k_rows(gate, own):
    n = gate.shape[0]
    row = lax.broadcasted_iota(jnp.int32, gate.shape, 0).astype(_F32)
    past = row < own
    gate = jnp.where(past, gate, NEG)
    mask = jnp.where(row == own, 0.0, NEG)
    for _ in range(min(TOP_K, n)):
        mx = jnp.max(gate, axis=0, keepdims=True)
        first = jnp.min(jnp.where(gate == mx, row, float(n)), axis=0, keepdims=True)
        chosen = row == first
        mask = jnp.where(chosen, jnp.where(past, 0.0, mask), mask)
        gate = jnp.where(chosen, REMOVED, gate)
    return mask


def _attn_prompt_kernel(qtz_ref, k_ref, vt_ref, kmean_ref, bias_ref, o_ref,
                        sel_ref, m_ref, l_ref, acc_ref, s_ref):
    i = pl.program_id(1)
    tq = qtz_ref.shape[3]
    own = i.astype(_F32)
    for h in range(N_HEADS):
        km = kmean_ref[0, :, (h // 2) * LANES:(h // 2 + 1) * LANES].astype(_BF)
        sel_ref[h] = _top_k_rows(_dot(km, qtz_ref[0, h]), own)
    m_ref[...] = jnp.full(m_ref.shape, NEG, _F32)
    l_ref[...] = jnp.zeros(l_ref.shape, _F32)
    acc_ref[...] = jnp.zeros(acc_ref.shape, _F32)

    n_pairs = N_HEADS // 2
    last_block = k_ref.shape[1] - 1

    def logits(j, pair):
        kj = k_ref[0, j, :, pair * LANES:(pair + 1) * LANES]
        qq = jnp.concatenate([qtz_ref[0, 2 * pair], qtz_ref[0, 2 * pair + 1]], axis=1)
        return _dot(kj, qq)

    s_ref[...] = logits(0, 0)

    def body(j, carry):
        d = jnp.minimum(i - j, N_BIAS_TILES - 1)
        s_next = s_ref[...]
        for pair in range(n_pairs):
            s2 = s_next
            if pair + 1 < n_pairs:
                s_next = logits(j, pair + 1)
            else:
                s_ref[...] = logits(jnp.minimum(j + 1, last_block), 0)
            for e in range(2):
                h = 2 * pair + e
                s = s2[:, e * tq:(e + 1) * tq] + bias_ref[h, d] + sel_ref[h, pl.ds(j, 1), :]
                m_old = m_ref[h:h + 1, :]
                m_new = jnp.maximum(m_old, jnp.max(s, axis=0, keepdims=True))
                alpha = jnp.exp(m_old - m_new)
                p = jnp.exp(s - m_new)
                m_ref[h:h + 1, :] = m_new
                l_ref[h:h + 1, :] = alpha * l_ref[h:h + 1, :] + jnp.sum(p, axis=0, keepdims=True)
                acc_ref[h] = alpha * acc_ref[h] + _dot(vt_ref[0, h, j], p.astype(_BF))
        return carry

    lax.fori_loop(0, i + 1, body, 0)
    for h in range(N_HEADS):
        o_ref[0, h * HEAD_DIM:(h + 1) * HEAD_DIM, :] = (acc_ref[h] / l_ref[h:h + 1, :]).astype(_BF)


def _attn_prompt(qtz, kbf, vt, kmean, bias_p):
    bsz, _, _, seq = qtz.shape
    nb = seq // BLOCK
    kb4 = kbf.reshape(bsz, nb, BLOCK, ATT_WIDTH)
    km3 = kmean.reshape(bsz, nb, ATT_WIDTH)
    per_batch = lambda shape: pl.BlockSpec(
        (1,) + shape[1:], lambda b, i: (b,) + (0,) * (len(shape) - 1), pipeline_mode=pl.Buffered(1))
    return pl.pallas_call(
        _attn_prompt_kernel,
        grid=(bsz, nb),
        in_specs=[
            pl.BlockSpec((1, N_HEADS, 2 * HEAD_DIM, BLOCK), lambda b, i: (b, 0, 0, i)),
            per_batch(kb4.shape), per_batch(vt.shape), per_batch(km3.shape),
            _const_spec(bias_p.shape),
        ],
        out_specs=pl.BlockSpec((1, ATT_WIDTH, BLOCK), lambda b, i: (b, 0, i)),
        out_shape=jax.ShapeDtypeStruct((bsz, ATT_WIDTH, seq), _BF),
        scratch_shapes=[
            pltpu.VMEM((N_HEADS, nb, BLOCK), _F32),
            pltpu.VMEM((N_HEADS, BLOCK), _F32),
            pltpu.VMEM((N_HEADS, BLOCK), _F32),
            pltpu.VMEM((N_HEADS, HEAD_DIM, BLOCK), _F32),
            pltpu.VMEM((BLOCK, 2 * BLOCK), _F32),
        ],
        compiler_params=_params(("arbitrary", "arbitrary")),
        name="attn_prompt",
    )(qtz, kb4, vt, km3, bias_p)


def _attn_sample_kernel(pt_ref, q_ref, kn_ref, vn_ref, bias_ref, *refs, n_pages):
    del pt_ref
    k_refs, v_refs, o_ref = refs[:n_pages], refs[n_pages:2 * n_pages], refs[2 * n_pages]
    rows = q_ref.shape[1]
    page_cols = PAGE_SIZE * N_HEADS
    pages_per_block = BLOCK // PAGE_SIZE
    n_blocks = n_pages // pages_per_block
    q = q_ref[0].astype(_BF)

    k_pages = [r[0, 0] for r in k_refs]
    sums = [jnp.sum(kp, axis=0) for kp in k_pages]
    kmean = jnp.concatenate(
        [sum(sums[n * pages_per_block:(n + 1) * pages_per_block]) for n in range(n_blocks)],
        axis=0) * (1.0 / BLOCK)
    gate = _dot_nt(q, kmean.astype(_BF))

    col = lax.broadcasted_iota(jnp.int32, gate.shape, 1)
    same_head = (col % N_HEADS) == (lax.broadcasted_iota(jnp.int32, gate.shape, 0) % N_HEADS)
    col = col.astype(_F32)
    gate = jnp.where(same_head, gate, REMOVED)
    sel = jnp.full(gate.shape, NEG, _F32)
    for _ in range(min(TOP_K, n_blocks)):
        mx = jnp.max(gate, axis=1, keepdims=True)
        first = jnp.min(jnp.where(gate == mx, col, float(gate.shape[1])), axis=1, keepdims=True)
        chosen = col == first
        sel = jnp.where(chosen, 0.0, sel)
        gate = jnp.where(chosen, REMOVED, gate)
    sel_block = [jnp.max(sel[:, n * N_HEADS:(n + 1) * N_HEADS], axis=1, keepdims=True)
                 for n in range(n_blocks)]

    logits = []
    for p in range(n_pages):
        k2 = k_pages[p].reshape(page_cols, HEAD_DIM).astype(_BF)
        s = _dot_nt(q, k2) + bias_ref[:, p * page_cols:(p + 1) * page_cols]
        logits.append(s + sel_block[p // pages_per_block])
    pad = jnp.zeros((LANES - rows, HEAD_DIM), _F32)
    k_own = jnp.concatenate([kn_ref[0], pad], axis=0).astype(_BF)
    v_own = jnp.concatenate([vn_ref[0], pad], axis=0).astype(_BF)
    logits.append(_dot_nt(q, k_own) + bias_ref[:, n_pages * page_cols:])

    m = functools.reduce(jnp.maximum, [jnp.max(s, axis=1, keepdims=True) for s in logits])
    l = jnp.zeros((rows, 1), _F32)
    acc = jnp.zeros((rows, HEAD_DIM), _F32)
    for p in range(n_pages + 1):
        pr = jnp.exp(logits[p] - m)
        l = l + jnp.sum(pr, axis=1, keepdims=True)
        vp = v_own if p == n_pages else v_refs[p][0, 0].reshape(page_cols, HEAD_DIM).astype(_BF)
        acc = acc + _dot(pr.astype(_BF), vp)
    o_ref[0] = acc / l


def _attn_sample(page_table, q, k_new, v_new, bias_s, cache_k, cache_v, layer):
    n_seq, n_pages = page_table.shape
    page_spec = lambda p: pl.BlockSpec(
        (1, 1, PAGE_SIZE, N_HEADS, HEAD_DIM), lambda b, pt, p=p: (layer, pt[b, p], 0, 0, 0))
    per_seq = pl.BlockSpec((1,) + q.shape[1:], lambda b, pt: (b, 0, 0))
    grid_spec = pltpu.PrefetchScalarGridSpec(
        num_scalar_prefetch=1,
        grid=(n_seq,),
        in_specs=[per_seq, per_seq, per_seq,
                  pl.BlockSpec(bias_s.shape, lambda b, pt: (0, 0), pipeline_mode=pl.Buffered(1))]
                 + [page_spec(p) for p in range(n_pages)] * 2,
        out_specs=per_seq,
    )
    return pl.pallas_call(
        functools.partial(_attn_sample_kernel, n_pages=n_pages),
        grid_spec=grid_spec,
        out_shape=jax.ShapeDtypeStruct(q.shape, _F32),
        compiler_params=_params(("arbitrary",)),
        name="attn_sample",
    )(page_table, q, k_new, v_new, bias_s, *([cache_k] * n_pages), *([cache_v] * n_pages))


def _sample_bias_table(bias_s8, n_q, past_len):
    rows = n_q * N_HEADS
    eye = jnp.eye(N_HEADS, dtype=bool)[None, :, None, :]
    by_query = bias_s8[:, :n_q].transpose(1, 0, 2)
    spread = jnp.where(eye, by_query[..., None], NEG)
    past = spread[:, :, :past_len].reshape(rows, past_len * N_HEADS)
    own = spread[:, :, past_len:past_len + n_q].reshape(rows, rows)
    own = jnp.pad(own, ((0, 0), (0, LANES - rows)), constant_values=NEG)
    return jnp.concatenate([past, own], axis=1)


def _post_kernel(x_ref, att_ref, cb_ref, ga_ref, gb_ref, mod_ref, g_ref, wa_ref, wc_ref, wo_ref,
                 x1_ref, h2_ref, *, reps, att_transposed):
    if att_transposed:
        att = jnp.transpose(att_ref[0].astype(_F32)).astype(_BF)
    else:
        att = att_ref[0].astype(_BF)
    y_a = _dot(att, wa_ref[...])
    y_b = _dot(cb_ref[0], wc_ref[...])
    merged = ga_ref[0] * y_a + gb_ref[0] * y_b
    x1 = x_ref[0] + _mod_chunk(mod_ref, 2, reps) * _dot(merged.astype(_BF), wo_ref[...])
    x1_ref[0] = x1
    h2 = _rmsnorm(x1, g_ref[...]) * (1.0 + _mod_chunk(mod_ref, 4, reps)) + _mod_chunk(mod_ref, 3, reps)
    h2_ref[0] = h2.astype(_BF)


def _post(x, att, cb, ga, gb, mod, g, wa, wc, wo, *, tr, att_transposed):
    bsz, seq, _ = x.shape
    nt = seq // tr
    reps = 1 if mod.shape[1] == 1 else tr // mod.shape[1]
    row = lambda w: pl.BlockSpec((1, tr, w), lambda b, t: (b, t, 0))
    att_spec = (pl.BlockSpec((1, ATT_WIDTH, tr), lambda b, t: (b, 0, t)) if att_transposed
                else row(ATT_WIDTH))
    return pl.pallas_call(
        functools.partial(_post_kernel, reps=reps, att_transposed=att_transposed),
        grid=(bsz, nt),
        in_specs=[
            row(D_MODEL), att_spec, row(CONV_WIDTH), row(D_MODEL), row(D_MODEL),
            pl.BlockSpec((1,) + mod.shape[1:], lambda b, t: (b, 0, 0)),
            _const_spec((1, D_MODEL)),
            _const_spec(wa.shape), _const_spec(wc.shape), _const_spec(wo.shape),
        ],
        out_specs=[row(D_MODEL), row(D_MODEL)],
        out_shape=[jax.ShapeDtypeStruct(x.shape, _F32), jax.ShapeDtypeStruct(x.shape, _BF)],
        compiler_params=_params(("arbitrary", "arbitrary")),
        name="post_attn",
    )(x, att, cb, ga, gb, mod, g, wa, wc, wo)


def _ffn_kernel(*refs, reps, stride, final):
    if stride == 1:
        x1_ref, h2_ref, mod_ref, wu_ref, cw_ref, cbias_ref, wd_ref = refs[:7]
        rest = refs[7:]
        prev_ref = None
    else:
        x1_ref, h2_ref, mod_ref, wu_ref, cw_ref, cbias_ref, wd_ref, prev_ref = refs[:8]
        rest = refs[8:]
    if final:
        fg_ref, rest = rest[0], rest[1:]
    out_ref, state_ref = rest[0], rest[1]
    ext_ref = rest[2] if stride == 1 else None

    rows = x1_ref.shape[1]
    hb = h2_ref[0]
    cw = cw_ref[...]
    cbias = cbias_ref[...]

    if stride == 1:
        @pl.when(pl.program_id(1) == 0)
        def _():
            ext_ref[0:8, :] = jnp.zeros((8, 2 * D_FF), _F32)

    def conv_cols(sl):
        up = _dot(hb, wu_ref[:, sl])
        if stride == 1:
            ext_ref[8:8 + rows, sl] = up
            e0, e1, e2 = ext_ref[6:6 + rows, sl], ext_ref[7:7 + rows, sl], up
        else:
            ext = jnp.concatenate([prev_ref[:, sl], up], axis=0)
            e0, e1, e2 = ext[0:rows], ext[stride:stride + rows], up
            state_ref[:, sl] = ext[rows:rows + 2 * stride]
        return cw[0:1, sl] * e0 + cw[1:2, sl] * e1 + cw[2:3, sl] * e2 + cbias[:, sl]

    acc = jnp.zeros((rows, D_MODEL), _F32)
    for c in range(D_FF // FFN_CHUNK):
        a = conv_cols(slice(c * FFN_CHUNK, (c + 1) * FFN_CHUNK))
        b = conv_cols(slice(D_FF + c * FFN_CHUNK, D_FF + (c + 1) * FFN_CHUNK))
        acc = acc + _dot((_silu(a) * b).astype(_BF), wd_ref[c * FFN_CHUNK:(c + 1) * FFN_CHUNK, :])

    if stride == 1:
        state_ref[0] = ext_ref[rows + 6:rows + 8, :]
        ext_ref[0:8, :] = ext_ref[rows:rows + 8, :]

    x2 = x1_ref[0] + _mod_chunk(mod_ref, 5, reps) * acc
    out_ref[0] = _rmsnorm(x2, fg_ref[...]) if final else x2


def _ffn(x1, h2, mod, wu, cw, cbias, wd, prev, final_g, *, tr, stride):
    bsz, seq, _ = x1.shape
    nt = seq // tr
    reps = 1 if mod.shape[1] == 1 else tr // mod.shape[1]
    final = final_g is not None
    row = lambda w: pl.BlockSpec((1, tr, w), lambda b, t: (b, t, 0))
    in_specs = [row(D_MODEL), row(D_MODEL),
                pl.BlockSpec((1,) + mod.shape[1:], lambda b, t: (b, 0, 0)),
                _const_spec(wu.shape), _const_spec(cw.shape), _const_spec(cbias.shape), _const_spec(wd.shape)]
    args = [x1, h2, mod, wu, cw, cbias, wd]
    if stride == 1:
        state_shape = jax.ShapeDtypeStruct((bsz, 2, 2 * D_FF), _F32)
        state_spec = pl.BlockSpec((1, 2, 2 * D_FF), lambda b, t: (b, 0, 0))
        scratch = [pltpu.VMEM((tr + 8, 2 * D_FF), _F32)]
    else:
        assert bsz == 1 and nt == 1 and tr % stride == 0
        in_specs.append(_const_spec(prev.shape))
        args.append(prev)
        state_shape = jax.ShapeDtypeStruct(prev.shape, _F32)
        state_spec = pl.BlockSpec(prev.shape, lambda b, t: (0, 0))
        scratch = []
    if final:
        in_specs.append(_const_spec(final_g.shape))
        args.append(final_g)
    return pl.pallas_call(
        functools.partial(_ffn_kernel, reps=reps, stride=stride, final=final),
        grid=(bsz, nt),
        in_specs=in_specs,
        out_specs=[row(D_MODEL), state_spec],
        out_shape=[jax.ShapeDtypeStruct(x1.shape, _F32), state_shape],
        scratch_shapes=scratch,
        compiler_params=_params(("arbitrary", "arbitrary")),
        name="conv_ffn",
    )(*args)


def kernel(x_prompt, x_sample, c_prompt, c_sample, cache_k, cache_v, state_conv, state_ffn, page_table,
           rel_bias, ada_w, ada_b, norm_mix_g, norm_ffn_g, final_norm_g, w_in, conv_w, w_att_out,
           w_conv_out, w_o, w_up, ffn_conv_w, ffn_conv_b, w_down):
    depth = w_in.shape[0]
    bsz, seq, _ = x_prompt.shape
    n_seq, n_q, _ = x_sample.shape
    n_pages = page_table.shape[1]
    past_len = n_pages * PAGE_SIZE
    assert seq % BLOCK == 0 and past_len % BLOCK == 0 and n_q <= 8 and bsz <= 8
    assert cache_k.shape[2] == PAGE_SIZE

    pad_rows = 8
    c_all = jnp.concatenate([c_prompt, jnp.zeros((pad_rows - bsz, D_MODEL), _F32), c_sample], axis=0)
    mod_all = _modulation(c_all, ada_w, ada_b)
    mod_p = mod_all[:, :bsz].reshape(depth, bsz, 1, -1)
    mod_s = mod_all[:, pad_rows:].reshape(depth, 1, n_seq, -1)

    bias_p, bias_s8 = _bias_tables(rel_bias, past_len)
    bias_s = _sample_bias_table(bias_s8, n_q, past_len)

    time_major = lambda a: a.transpose(1, 0, 2).reshape(-1, a.shape[2])
    seq_major = lambda a, steps: a.reshape(steps, n_seq, -1).transpose(1, 0, 2)
    rows_s = n_q * n_seq

    xp = x_prompt
    xs = time_major(x_sample)[None]
    fin = final_norm_g.reshape(1, D_MODEL)
    outs = [[] for _ in range(8)]
    for l in range(depth):
        g_mix = norm_mix_g[l].reshape(1, D_MODEL)
        g_ffn = norm_ffn_g[l].reshape(1, D_MODEL)
        w_in_bf = w_in[l].astype(_BF)
        wa, wc, wo = w_att_out[l].astype(_BF), w_conv_out[l].astype(_BF), w_o[l].astype(_BF)
        wu, wd = w_up[l].astype(_BF), w_down[l].astype(_BF)
        cbias = ffn_conv_b[l].reshape(1, -1)
        final_g = fin if l == depth - 1 else None

        k, v, kbf, qtz, vt, kmean, cb, ga, gb, cstate = _inproj_prompt(xp, mod_p[l], g_mix, w_in_bf, conv_w[l])
        att_t = _attn_prompt(qtz, kbf, vt, kmean, bias_p)
        x1, h2 = _post(xp, att_t, cb, ga, gb, mod_p[l], g_ffn, wa, wc, wo, tr=BLOCK, att_transposed=True)
        xp, fstate = _ffn(x1, h2, mod_p[l], wu, ffn_conv_w[l], cbias, wd, None, final_g, tr=BLOCK, stride=1)
        outs[0].append(k.reshape(bsz, seq, N_HEADS, HEAD_DIM))
        outs[1].append(v.reshape(bsz, seq, N_HEADS, HEAD_DIM))
        outs[2].append(cstate)
        outs[3].append(fstate)

        q_s, k_s, v_s, cb_s, ga_s, gb_s, cstate_s = _inproj_sample(
            xs[0], mod_s[l], g_mix, w_in_bf, conv_w[l], time_major(state_conv[l]), n_q)
        k_sm, v_sm = seq_major(k_s, n_q), seq_major(v_s, n_q)
        by_head = lambda a: a.reshape(n_seq, n_q * N_HEADS, HEAD_DIM)
        att_s = _attn_sample(page_table, by_head(seq_major(q_s, n_q)), by_head(k_sm), by_head(v_sm),
                             bias_s, cache_k, cache_v, l)
        x1s, h2s = _post(xs, time_major(att_s.reshape(n_seq, n_q, ATT_WIDTH))[None], cb_s[None], ga_s[None], gb_s[None], mod_s[l], g_ffn,
                         wa, wc, wo, tr=rows_s, att_transposed=False)
        xs, fstate_s = _ffn(x1s, h2s, mod_s[l], wu, ffn_conv_w[l], cbias, wd, time_major(state_ffn[l]),
                            final_g, tr=rows_s, stride=n_seq)
        outs[4].append(k_sm.reshape(n_seq, n_q, N_HEADS, HEAD_DIM))
        outs[5].append(v_sm.reshape(n_seq, n_q, N_HEADS, HEAD_DIM))
        outs[6].append(seq_major(cstate_s, 2))
        outs[7].append(seq_major(fstate_s, 2))

    y_sample = seq_major(xs[0], n_q)
    return (xp, y_sample) + tuple(jnp.stack(o) for o in outs)
```

```python
import functools

import jax
import jax.numpy as jnp
from jax import lax
from jax.experimental import pallas as pl
from jax.experimental.pallas import tpu as pltpu

D_MODEL = 1024
N_HEADS = 8
HEAD_DIM = 64
ATT_WIDTH = N_HEADS * HEAD_DIM
CONV_WIDTH = 512
BLOCK = 256
TOP_K = 3
N_BUCKETS = 32
MAX_DIST = 1024
D_FF = 2816
PAGE_SIZE = 128
EPS = 1e-6
NEG = -1e30
REMOVED = -3e38
LOG2E = 1.4426950408889634
N_IN = 3 * ATT_WIDTH + 3 * CONV_WIDTH + 2 * D_MODEL
LANES = 128
FFN_CHUNK = 256
N_BIAS_TILES = 6
VMEM_LIMIT = 56 * 1024 * 1024

_BF = jnp.bfloat16
_F32 = jnp.float32


def _bucket_upper_bounds():
    max_exact = N_BUCKETS // 2
    n_log = N_BUCKETS - max_exact
    ratio = MAX_DIST // max_exact
    assert ratio * max_exact == MAX_DIST
    ups = [b + 1 for b in range(max_exact)]
    for k in range(1, n_log):
        target = (max_exact ** n_log) * (ratio ** k)
        n = max_exact
        while n ** n_log < target:
            n += 1
        ups.append(n)
    return ups


_BUCKET_UPPER = _bucket_upper_bounds()
assert (N_BIAS_TILES - 2) * BLOCK + 1 >= _BUCKET_UPPER[-1]


def _dot(a, b):
    return jnp.dot(a, b, preferred_element_type=_F32)


def _dot_nt(a, b):
    return lax.dot_general(a, b, (((1,), (1,)), ((), ())), preferred_element_type=_F32)


def _sigmoid(x):
    return 1.0 / (1.0 + jnp.exp(-x))


def _silu(x):
    return x * _sigmoid(x)


def _rmsnorm(x, g):
    return x * lax.rsqrt(jnp.mean(x * x, axis=-1, keepdims=True) + EPS) * g


def _mod_chunk(mod_ref, idx, reps):
    v = mod_ref[0, :, idx * D_MODEL:(idx + 1) * D_MODEL]
    if reps > 1:
        v = jnp.concatenate([v] * reps, axis=0)
    return v


def _params(sem, vmem=VMEM_LIMIT):
    return pltpu.CompilerParams(dimension_semantics=sem, vmem_limit_bytes=vmem)


def _const_spec(shape):
    nd = len(shape)
    return pl.BlockSpec(shape, lambda *_: (0,) * nd, pipeline_mode=pl.Buffered(1))


def _mod_kernel(c_ref, w_ref, b_ref, o_ref):
    a = _silu(c_ref[...]).astype(_BF)
    o_ref[0] = _dot(a, w_ref[0].astype(_BF)) + b_ref[0]


def _modulation(c_all, ada_w, ada_b):
    depth = ada_w.shape[0]
    rows = c_all.shape[0]
    n_chunks = ada_w.shape[2] // D_MODEL
    return pl.pallas_call(
        _mod_kernel,
        grid=(depth, n_chunks),
        in_specs=[
            pl.BlockSpec((rows, D_MODEL), lambda l, n: (0, 0)),
            pl.BlockSpec((1, D_MODEL, D_MODEL), lambda l, n: (l, 0, n)),
            pl.BlockSpec((1, 1, D_MODEL), lambda l, n: (l, 0, n)),
        ],
        out_specs=pl.BlockSpec((1, rows, D_MODEL), lambda l, n: (l, 0, n)),
        out_shape=jax.ShapeDtypeStruct((depth, rows, ada_w.shape[2]), _F32),
        compiler_params=_params(("arbitrary", "arbitrary")),
        name="modulation",
    )(c_all, ada_w, ada_b.reshape(depth, 1, -1))


def _bias_kernel(rb_ref, bp_ref, bs_ref, *, past_len):
    h = pl.program_id(0)

    def table(dist, scale):
        n = jnp.maximum(dist, 0)
        v = jnp.full(dist.shape, rb_ref[N_BUCKETS - 1, h], _F32)
        for b in range(N_BUCKETS - 2, -1, -1):
            v = jnp.where(n < _BUCKET_UPPER[b], rb_ref[b, h], v)
        return jnp.where(dist >= 0, v * scale, NEG)

    kk = lax.broadcasted_iota(jnp.int32, (BLOCK, BLOCK), 0)
    qq = lax.broadcasted_iota(jnp.int32, (BLOCK, BLOCK), 1)
    for d in range(N_BIAS_TILES):
        bp_ref[0, d] = table(d * BLOCK + qq - kk, LOG2E)
    t = lax.broadcasted_iota(jnp.int32, bs_ref.shape[1:], 0)
    p = lax.broadcasted_iota(jnp.int32, bs_ref.shape[1:], 1)
    bs_ref[0] = table(past_len + t - p, 1.0)


def _bias_tables(rel_bias, past_len):
    s_cols = past_len + PAGE_SIZE
    return pl.pallas_call(
        functools.partial(_bias_kernel, past_len=past_len),
        grid=(N_HEADS,),
        in_specs=[pl.BlockSpec(memory_space=pltpu.SMEM)],
        out_specs=[
            pl.BlockSpec((1, N_BIAS_TILES, BLOCK, BLOCK), lambda h: (h, 0, 0, 0)),
            pl.BlockSpec((1, 8, s_cols), lambda h: (h, 0, 0)),
        ],
        out_shape=[
            jax.ShapeDtypeStruct((N_HEADS, N_BIAS_TILES, BLOCK, BLOCK), _F32),
            jax.ShapeDtypeStruct((N_HEADS, 8, s_cols), _F32),
        ],
        compiler_params=_params(("arbitrary",)),
        name="bias_tables",
    )(rel_bias)


def _project(x, g, sh, sc, w_ref):
    hb = (_rmsnorm(x, g) * (1.0 + sc) + sh).astype(_BF)
    bounds = (0, ATT_WIDTH, 2 * ATT_WIDTH, 3 * ATT_WIDTH, 3 * ATT_WIDTH + CONV_WIDTH,
              3 * ATT_WIDTH + 2 * CONV_WIDTH, 3 * ATT_WIDTH + 3 * CONV_WIDTH,
              3 * ATT_WIDTH + 3 * CONV_WIDTH + D_MODEL, N_IN)
    return [_dot(hb, w_ref[:, a:b]) for a, b in zip(bounds[:-1], bounds[1:])]


def _inproj_prompt_kernel(x_ref, mod_ref, g_ref, w_ref, cw_ref,
                          k_ref, v_ref, kbf_ref, qtz_ref, vt_ref, kmean_ref, cb_ref, ga_ref, gb_ref,
                          cstate_ref, ext_ref):
    tr = x_ref.shape[1]
    t = pl.program_id(1)

    @pl.when(t == 0)
    def _():
        ext_ref[0:8, :] = jnp.zeros((8, CONV_WIDTH), _F32)

    q, k, v, u, bg, cg, ga, gb = _project(
        x_ref[0], g_ref[...], _mod_chunk(mod_ref, 0, 1), _mod_chunk(mod_ref, 1, 1), w_ref)

    qt = jnp.transpose(q * (HEAD_DIM ** -0.5))
    zeros = jnp.zeros((HEAD_DIM, tr), _F32)
    for copy, scale in enumerate((1.0, LOG2E)):
        for h in range(N_HEADS):
            qh = qt[h * HEAD_DIM:(h + 1) * HEAD_DIM] * scale
            pair = [qh, zeros] if h % 2 == 0 else [zeros, qh]
            qtz_ref[0, copy, h] = jnp.concatenate(pair, axis=0).astype(_BF)

    k_ref[0] = k
    kbf_ref[0] = k.astype(_BF)
    kmean_ref[0, 0] = jnp.mean(k, axis=0, keepdims=True)
    v_ref[0] = v
    vt = jnp.transpose(v)
    for h in range(N_HEADS):
        vt_ref[0, h, 0] = vt[h * HEAD_DIM:(h + 1) * HEAD_DIM].astype(_BF)

    ext_ref[8:8 + tr, :] = cg * u
    cw = cw_ref[...]
    conv = (cw[0:1] * ext_ref[6:6 + tr, :] + cw[1:2] * ext_ref[7:7 + tr, :]
            + cw[2:3] * ext_ref[8:8 + tr, :])
    cb_ref[0] = (bg * conv).astype(_BF)
    cstate_ref[0] = ext_ref[tr + 6:tr + 8, :]
    ext_ref[0:8, :] = ext_ref[tr:tr + 8, :]

    ga_ref[0] = _sigmoid(ga)
    gb_ref[0] = _sigmoid(gb)


def _inproj_prompt(x, mod, g, w_bf, cw):
    bsz, seq, _ = x.shape
    tr = BLOCK
    nt = seq // tr
    row = lambda w: pl.BlockSpec((1, tr, w), lambda b, t: (b, t, 0))
    return pl.pallas_call(
        _inproj_prompt_kernel,
        grid=(bsz, nt),
        in_specs=[
            row(D_MODEL),
            pl.BlockSpec((1, 1, mod.shape[2]), lambda b, t: (b, 0, 0)),
            _const_spec((1, D_MODEL)),
            _const_spec((D_MODEL, N_IN)),
            _const_spec((3, CONV_WIDTH)),
        ],
        out_specs=[
            row(ATT_WIDTH), row(ATT_WIDTH), row(ATT_WIDTH),
            pl.BlockSpec((1, 2, N_HEADS, 2 * HEAD_DIM, tr), lambda b, t: (b, 0, 0, 0, t)),
            pl.BlockSpec((1, N_HEADS, 1, HEAD_DIM, tr), lambda b, t: (b, 0, t, 0, 0)),
            pl.BlockSpec((1, 1, 1, ATT_WIDTH), lambda b, t: (b, t, 0, 0)),
            row(CONV_WIDTH), row(D_MODEL), row(D_MODEL),
            pl.BlockSpec((1, 2, CONV_WIDTH), lambda b, t: (b, 0, 0)),
        ],
        out_shape=[
            jax.ShapeDtypeStruct((bsz, seq, ATT_WIDTH), _F32),
            jax.ShapeDtypeStruct((bsz, seq, ATT_WIDTH), _F32),
            jax.ShapeDtypeStruct((bsz, seq, ATT_WIDTH), _BF),
            jax.ShapeDtypeStruct((bsz, 2, N_HEADS, 2 * HEAD_DIM, seq), _BF),
            jax.ShapeDtypeStruct((bsz, N_HEADS, nt, HEAD_DIM, tr), _BF),
            jax.ShapeDtypeStruct((bsz, nt, 1, ATT_WIDTH), _F32),
            jax.ShapeDtypeStruct((bsz, seq, CONV_WIDTH), _BF),
            jax.ShapeDtypeStruct((bsz, seq, D_MODEL), _F32),
            jax.ShapeDtypeStruct((bsz, seq, D_MODEL), _F32),
            jax.ShapeDtypeStruct((bsz, 2, CONV_WIDTH), _F32),
        ],
        scratch_shapes=[pltpu.VMEM((tr + 8, CONV_WIDTH), _F32)],
        compiler_params=_params(("arbitrary", "arbitrary")),
        name="inproj_prompt",
    )(x, mod, g, w_bf, cw)


def _inproj_sample_kernel(x_ref, mod_ref, g_ref, w_ref, cw_ref, prev_ref,
                          q_ref, k_ref, v_ref, cb_ref, ga_ref, gb_ref, cstate_ref, *, reps):
    rows = x_ref.shape[0]
    stride = rows // reps
    q, k, v, u, bg, cg, ga, gb = _project(
        x_ref[...], g_ref[...], _mod_chunk(mod_ref, 0, reps), _mod_chunk(mod_ref, 1, reps), w_ref)
    q_ref[...] = q * (HEAD_DIM ** -0.5)
    k_ref[...] = k
    v_ref[...] = v
    cu = cg * u
    ext = jnp.concatenate([prev_ref[...], cu], axis=0)
    cw = cw_ref[...]
    conv = (cw[0:1] * ext[0:rows] + cw[1:2] * ext[stride:stride + rows]
            + cw[2:3] * ext[2 * stride:2 * stride + rows])
    cb_ref[...] = (bg * conv).astype(_BF)
    cstate_ref[...] = ext[rows:rows + 2 * stride]
    ga_ref[...] = _sigmoid(ga)
    gb_ref[...] = _sigmoid(gb)


def _inproj_sample(x, mod, g, w_bf, cw, prev, reps):
    rows = x.shape[0]
    full = lambda a: pl.BlockSpec(a.shape, lambda i: (0,) * a.ndim)
    shp = lambda w, dt=_F32: jax.ShapeDtypeStruct((rows, w), dt)
    outs = [shp(ATT_WIDTH), shp(ATT_WIDTH), shp(ATT_WIDTH), shp(CONV_WIDTH, _BF),
            shp(D_MODEL), shp(D_MODEL), jax.ShapeDtypeStruct(prev.shape, _F32)]
    return pl.pallas_call(
        functools.partial(_inproj_sample_kernel, reps=reps),
        grid=(1,),
        in_specs=[full(x), full(mod), full(g), full(w_bf), full(cw), full(prev)],
        out_specs=[pl.BlockSpec(o.shape, lambda i: (0, 0)) for o in outs],
        out_shape=outs,
        compiler_params=_params(("arbitrary",)),
        name="inproj_sample",
    )(x, mod, g, w_bf, cw, prev)


def _top_k_rows(gate, own):
    n = gate.shape[0]
    row = lax.broadcasted_iota(jnp.int32, gate.shape, 0).astype(_F32)
    past = row < own
    gate = jnp.where(past, gate, NEG)
    mask = jnp.where(row == own, 0.0, NEG)
    for _ in range(min(TOP_K, n)):
        mx = jnp.max(gate, axis=0, keepdims=True)
        first = jnp.min(jnp.where(gate == mx, row, float(n)), axis=0, keepdims=True)
        chosen = row == first
        mask = jnp.where(chosen, jnp.where(past, 0.0, mask), mask)
        gate = jnp.where(chosen, REMOVED, gate)
    return mask


def _attn_prompt_kernel(qtz_ref, k_ref, vt_ref, kmean_ref, bias_ref, o_ref,
                        sel_ref, m_ref, l_ref, acc_ref, s_ref):
    i = pl.program_id(1)
    tq = qtz_ref.shape[4]
    own = i.astype(_F32)
    n_far = jnp.maximum(i - (N_BIAS_TILES - 2), 0)
    is_far = lax.broadcasted_iota(jnp.int32, sel_ref.shape[1:], 0) < n_far
    for h in range(N_HEADS):
        km = kmean_ref[0, :, (h // 2) * LANES:(h // 2 + 1) * LANES].astype(_BF)
        mask = _top_k_rows(_dot(km, qtz_ref[0, 0, h]), own)
        far_bias = bias_ref[h, N_BIAS_TILES - 1, 0:1, 0:1]
        sel_ref[h] = mask + jnp.where(is_far, far_bias, 0.0)
    m_ref[...] = jnp.full(m_ref.shape, NEG, _F32)
    l_ref[...] = jnp.zeros(l_ref.shape, _F32)
    acc_ref[...] = jnp.zeros(acc_ref.shape, _F32)

    n_pairs = N_HEADS // 2
    last_block = k_ref.shape[1] - 1

    def logits(j, pair):
        kj = k_ref[0, j, :, pair * LANES:(pair + 1) * LANES]
        qq = jnp.concatenate([qtz_ref[0, 1, 2 * pair], qtz_ref[0, 1, 2 * pair + 1]], axis=1)
        return _dot(kj, qq)

    s_ref[...] = logits(0, 0)

    def body(j, carry, near):
        s_next = s_ref[...]
        for pair in range(n_pairs):
            s2 = s_next
            if pair + 1 < n_pairs:
                s_next = logits(j, pair + 1)
            else:
                s_ref[...] = logits(jnp.minimum(j + 1, last_block), 0)
            for e in range(2):
                h = 2 * pair + e
                s = s2[:, e * tq:(e + 1) * tq] + sel_ref[h, pl.ds(j, 1), :]
                if near:
                    s = s + bias_ref[h, i - j]
                m_old = m_ref[h:h + 1, :]
                m_new = jnp.maximum(m_old, jnp.max(s, axis=0, keepdims=True))
                alpha = jnp.exp2(m_old - m_new)
                p = jnp.exp2(s - m_new)
                m_ref[h:h + 1, :] = m_new
                l_ref[h:h + 1, :] = alpha * l_ref[h:h + 1, :] + jnp.sum(p, axis=0, keepdims=True)
                acc_ref[h] = alpha * acc_ref[h] + _dot(vt_ref[0, h, j], p.astype(_BF))
        return carry

    lax.fori_loop(0, n_far, functools.partial(body, near=False), 0)
    lax.fori_loop(n_far, i + 1, functools.partial(body, near=True), 0)
    for h in range(N_HEADS):
        o_ref[0, h * HEAD_DIM:(h + 1) * HEAD_DIM, :] = (acc_ref[h] / l_ref[h:h + 1, :]).astype(_BF)


def _attn_prompt(qtz, kbf, vt, kmean, bias_p):
    bsz, _, _, _, seq = qtz.shape
    nb = seq // BLOCK
    kb4 = kbf.reshape(bsz, nb, BLOCK, ATT_WIDTH)
    km3 = kmean.reshape(bsz, nb, ATT_WIDTH)
    per_batch = lambda shape: pl.BlockSpec(
        (1,) + shape[1:], lambda b, i: (b,) + (0,) * (len(shape) - 1), pipeline_mode=pl.Buffered(1))
    return pl.pallas_call(
        _attn_prompt_kernel,
        grid=(bsz, nb),
        in_specs=[
            pl.BlockSpec((1, 2, N_HEADS, 2 * HEAD_DIM, BLOCK), lambda b, i: (b, 0, 0, 0, i)),
            per_batch(kb4.shape), per_batch(vt.shape), per_batch(km3.shape),
            _const_spec(bias_p.shape),
        ],
        out_specs=pl.BlockSpec((1, ATT_WIDTH, BLOCK), lambda b, i: (b, 0, i)),
        out_shape=jax.ShapeDtypeStruct((bsz, ATT_WIDTH, seq), _BF),
        scratch_shapes=[
            pltpu.VMEM((N_HEADS, nb, BLOCK), _F32),
            pltpu.VMEM((N_HEADS, BLOCK), _F32),
            pltpu.VMEM((N_HEADS, BLOCK), _F32),
            pltpu.VMEM((N_HEADS, HEAD_DIM, BLOCK), _F32),
            pltpu.VMEM((BLOCK, 2 * BLOCK), _F32),
        ],
        compiler_params=_params(("arbitrary", "arbitrary")),
        name="attn_prompt",
    )(qtz, kb4, vt, km3, bias_p)


def _attn_sample_kernel(pt_ref, q_ref, kn_ref, vn_ref, bias_ref, *refs, n_pages, n_q):
    del pt_ref
    k_refs, v_refs, o_ref = refs[:n_pages], refs[n_pages:2 * n_pages], refs[2 * n_pages]
    rows = n_q * N_HEADS
    pages_per_block = BLOCK // PAGE_SIZE
    n_blocks = n_pages // pages_per_block

    q = q_ref[0]
    rep = jnp.concatenate([jnp.broadcast_to(q[t:t + 1], (N_HEADS, ATT_WIDTH)) for t in range(n_q)], axis=0)
    r_head = lax.broadcasted_iota(jnp.int32, (rows, ATT_WIDTH), 0) % N_HEADS
    c_head = lax.broadcasted_iota(jnp.int32, (rows, ATT_WIDTH), 1) // HEAD_DIM
    head_lanes = r_head == c_head
    q_rows = jnp.where(head_lanes, rep, 0.0).astype(_BF)

    kt_pages = [r[0, 0] for r in k_refs]
    block_sums = [
        jnp.sum(sum(kt_pages[n * pages_per_block:(n + 1) * pages_per_block]), axis=1, keepdims=True)
        for n in range(n_blocks)]
    kmean_t = jnp.concatenate(block_sums, axis=1) * (1.0 / BLOCK)
    gate = _dot(q_rows, kmean_t.astype(_BF))

    col = lax.broadcasted_iota(jnp.int32, gate.shape, 1).astype(_F32)
    sel = jnp.full(gate.shape, NEG, _F32)
    for _ in range(min(TOP_K, n_blocks)):
        mx = jnp.max(gate, axis=1, keepdims=True)
        first = jnp.min(jnp.where(gate == mx, col, float(n_blocks)), axis=1, keepdims=True)
        chosen = col == first
        sel = jnp.where(chosen, 0.0, sel)
        gate = jnp.where(chosen, REMOVED, gate)

    logits = []
    for p in range(n_pages):
        s = _dot(q_rows, kt_pages[p].astype(_BF)) + bias_ref[:, p * PAGE_SIZE:(p + 1) * PAGE_SIZE]
        n = p // pages_per_block
        logits.append(s + sel[:, n:n + 1])
    pad = jnp.zeros((PAGE_SIZE - kn_ref.shape[1], ATT_WIDTH), _F32)
    k_own = jnp.concatenate([kn_ref[0], pad], axis=0).astype(_BF)
    v_own = jnp.concatenate([vn_ref[0], pad], axis=0).astype(_BF)
    logits.append(_dot_nt(q_rows, k_own) + bias_ref[:, n_pages * PAGE_SIZE:])

    m = functools.reduce(jnp.maximum, logits)
    m = jnp.max(m, axis=1, keepdims=True)
    l = jnp.zeros((rows, 1), _F32)
    acc = jnp.zeros((rows, ATT_WIDTH), _F32)
    for p in range(n_pages + 1):
        pr = jnp.exp(logits[p] - m)
        l = l + jnp.sum(pr, axis=1, keepdims=True)
        if p == n_pages:
            acc = acc + _dot(pr.astype(_BF), v_own)
        else:
            acc = acc + _dot_nt(pr.astype(_BF), v_refs[p][0, 0].astype(_BF))
    out = jnp.where(head_lanes, acc / l, 0.0)
    o_ref[0] = jnp.sum(out.reshape(n_q, N_HEADS, ATT_WIDTH), axis=1)


def _attn_sample(page_table, q, k_new, v_new, bias_s, cache_kt, cache_vt, layer):
    n_seq, n_pages = page_table.shape
    n_q = q.shape[1]
    page_spec = lambda p: pl.BlockSpec(
        (1, 1, ATT_WIDTH, PAGE_SIZE), lambda b, pt, p=p: (layer, pt[b, p], 0, 0))
    per_seq = lambda a: pl.BlockSpec((1,) + a.shape[1:], lambda b, pt: (b, 0, 0))
    grid_spec = pltpu.PrefetchScalarGridSpec(
        num_scalar_prefetch=1,
        grid=(n_seq,),
        in_specs=[per_seq(q), per_seq(k_new), per_seq(v_new),
                  pl.BlockSpec(bias_s.shape, lambda b, pt: (0, 0), pipeline_mode=pl.Buffered(1))]
                 + [page_spec(p) for p in range(n_pages)] * 2,
        out_specs=pl.BlockSpec((1, n_q, ATT_WIDTH), lambda b, pt: (b, 0, 0)),
    )
    return pl.pallas_call(
        functools.partial(_attn_sample_kernel, n_pages=n_pages, n_q=n_q),
        grid_spec=grid_spec,
        out_shape=jax.ShapeDtypeStruct((n_seq, n_q, ATT_WIDTH), _F32),
        compiler_params=_params(("arbitrary",)),
        name="attn_sample",
    )(page_table, q, k_new, v_new, bias_s, *([cache_kt] * n_pages), *([cache_vt] * n_pages))


def _post_kernel(x_ref, att_ref, cb_ref, ga_ref, gb_ref, mod_ref, g_ref, wa_ref, wc_ref, wo_ref,
                 x1_ref, h2_ref, *, reps, att_transposed):
    if att_transposed:
        att = jnp.transpose(att_ref[0].astype(_F32)).astype(_BF)
    else:
        att = att_ref[0].astype(_BF)
    y_a = _dot(att, wa_ref[...])
    y_b = _dot(cb_ref[0], wc_ref[...])
    merged = ga_ref[0] * y_a + gb_ref[0] * y_b
    x1 = x_ref[0] + _mod_chunk(mod_ref, 2, reps) * _dot(merged.astype(_BF), wo_ref[...])
    x1_ref[0] = x1
    h2 = _rmsnorm(x1, g_ref[...]) * (1.0 + _mod_chunk(mod_ref, 4, reps)) + _mod_chunk(mod_ref, 3, reps)
    h2_ref[0] = h2.astype(_BF)


def _post(x, att, cb, ga, gb, mod, g, wa, wc, wo, *, tr, att_transposed):
    bsz, seq, _ = x.shape
    nt = seq // tr
    reps = 1 if mod.shape[1] == 1 else tr // mod.shape[1]
    row = lambda w: pl.BlockSpec((1, tr, w), lambda b, t: (b, t, 0))
    att_spec = (pl.BlockSpec((1, ATT_WIDTH, tr), lambda b, t: (b, 0, t)) if att_transposed
                else row(ATT_WIDTH))
    return pl.pallas_call(
        functools.partial(_post_kernel, reps=reps, att_transposed=att_transposed),
        grid=(bsz, nt),
        in_specs=[
            row(D_MODEL), att_spec, row(CONV_WIDTH), row(D_MODEL), row(D_MODEL),
            pl.BlockSpec((1,) + mod.shape[1:], lambda b, t: (b, 0, 0)),
            _const_spec((1, D_MODEL)),
            _const_spec(wa.shape), _const_spec(wc.shape), _const_spec(wo.shape),
        ],
        out_specs=[row(D_MODEL), row(D_MODEL)],
        out_shape=[jax.ShapeDtypeStruct(x.shape, _F32), jax.ShapeDtypeStruct(x.shape, _BF)],
        compiler_params=_params(("arbitrary", "arbitrary")),
        name="post_attn",
    )(x, att, cb, ga, gb, mod, g, wa, wc, wo)


def _ffn_kernel(*refs, reps, stride, final):
    if stride == 1:
        x1_ref, h2_ref, mod_ref, wu_ref, cw_ref, cbias_ref, wd_ref = refs[:7]
        rest = refs[7:]
        prev_ref = None
    else:
        x1_ref, h2_ref, mod_ref, wu_ref, cw_ref, cbias_ref, wd_ref, prev_ref = refs[:8]
        rest = refs[8:]
    if final:
        fg_ref, rest = rest[0], rest[1:]
    out_ref, state_ref = rest[0], rest[1]
    ext_ref = rest[2] if stride == 1 else None

    rows = x1_ref.shape[1]
    hb = h2_ref[0]
    cw = cw_ref[...]
    cbias = cbias_ref[...]

    if stride == 1:
        @pl.when(pl.program_id(1) == 0)
        def _():
            ext_ref[0:8, :] = jnp.zeros((8, 2 * D_FF), _F32)

    n_chunks = D_FF // FFN_CHUNK
    a_cols = lambda c: slice(c * FFN_CHUNK, (c + 1) * FFN_CHUNK)
    b_cols = lambda c: slice(D_FF + c * FFN_CHUNK, D_FF + (c + 1) * FFN_CHUNK)
    up_pair = lambda c: (_dot(hb, wu_ref[:, a_cols(c)]), _dot(hb, wu_ref[:, b_cols(c)]))

    def conv_cols(up, sl):
        if stride == 1:
            ext_ref[8:8 + rows, sl] = up
            e0, e1, e2 = ext_ref[6:6 + rows, sl], ext_ref[7:7 + rows, sl], up
        else:
            ext = jnp.concatenate([prev_ref[:, sl], up], axis=0)
            e0, e1, e2 = ext[0:rows], ext[stride:stride + rows], up
            state_ref[:, sl] = ext[rows:rows + 2 * stride]
        return cw[0:1, sl] * e0 + cw[1:2, sl] * e1 + cw[2:3, sl] * e2 + cbias[:, sl]

    acc = jnp.zeros((rows, D_MODEL), _F32)
    up_next = up_pair(0)
    for c in range(n_chunks):
        up_a, up_b = up_next
        if c + 1 < n_chunks:
            up_next = up_pair(c + 1)
        a = conv_cols(up_a, a_cols(c))
        b = conv_cols(up_b, b_cols(c))
        acc = acc + _dot((_silu(a) * b).astype(_BF), wd_ref[a_cols(c), :])

    if stride == 1:
        state_ref[0] = ext_ref[rows + 6:rows + 8, :]
        ext_ref[0:8, :] = ext_ref[rows:rows + 8, :]

    x2 = x1_ref[0] + _mod_chunk(mod_ref, 5, reps) * acc
    out_ref[0] = _rmsnorm(x2, fg_ref[...]) if final else x2


def _ffn(x1, h2, mod, wu, cw, cbias, wd, prev, final_g, *, tr, stride):
    bsz, seq, _ = x1.shape
    nt = seq // tr
    reps = 1 if mod.shape[1] == 1 else tr // mod.shape[1]
    final = final_g is not None
    row = lambda w: pl.BlockSpec((1, tr, w), lambda b, t: (b, t, 0))
    in_specs = [row(D_MODEL), row(D_MODEL),
                pl.BlockSpec((1,) + mod.shape[1:], lambda b, t: (b, 0, 0)),
                _const_spec(wu.shape), _const_spec(cw.shape), _const_spec(cbias.shape), _const_spec(wd.shape)]
    args = [x1, h2, mod, wu, cw, cbias, wd]
    if stride == 1:
        state_shape = jax.ShapeDtypeStruct((bsz, 2, 2 * D_FF), _F32)
        state_spec = pl.BlockSpec((1, 2, 2 * D_FF), lambda b, t: (b, 0, 0))
        scratch = [pltpu.VMEM((tr + 8, 2 * D_FF), _F32)]
    else:
        assert bsz == 1 and nt == 1 and tr % stride == 0
        in_specs.append(_const_spec(prev.shape))
        args.append(prev)
        state_shape = jax.ShapeDtypeStruct(prev.shape, _F32)
        state_spec = pl.BlockSpec(prev.shape, lambda b, t: (0, 0))
        scratch = []
    if final:
        in_specs.append(_const_spec(final_g.shape))
        args.append(final_g)
    return pl.pallas_call(
        functools.partial(_ffn_kernel, reps=reps, stride=stride, final=final),
        grid=(bsz, nt),
        in_specs=in_specs,
        out_specs=[row(D_MODEL), state_spec],
        out_shape=[jax.ShapeDtypeStruct(x1.shape, _F32), state_shape],
        scratch_shapes=scratch,
        compiler_params=_params(("arbitrary", "arbitrary")),
        name="conv_ffn",
    )(*args)


def kernel(x_prompt, x_sample, c_prompt, c_sample, cache_k, cache_v, state_conv, state_ffn, page_table,
           rel_bias, ada_w, ada_b, norm_mix_g, norm_ffn_g, final_norm_g, w_in, conv_w, w_att_out,
           w_conv_out, w_o, w_up, ffn_conv_w, ffn_conv_b, w_down):
    depth = w_in.shape[0]
    bsz, seq, _ = x_prompt.shape
    n_seq, n_q, _ = x_sample.shape
    n_pages = page_table.shape[1]
    past_len = n_pages * PAGE_SIZE
    assert seq % BLOCK == 0 and past_len % BLOCK == 0 and n_q <= 8 and bsz <= 8
    assert cache_k.shape[2] == PAGE_SIZE

    pad_rows = 8
    c_all = jnp.concatenate([c_prompt, jnp.zeros((pad_rows - bsz, D_MODEL), _F32), c_sample], axis=0)
    mod_all = _modulation(c_all, ada_w, ada_b)
    mod_p = mod_all[:, :bsz].reshape(depth, bsz, 1, -1)
    mod_s = mod_all[:, pad_rows:].reshape(depth, 1, n_seq, -1)

    bias_p, bias_s8 = _bias_tables(rel_bias, past_len)
    bias_s = bias_s8[:, :n_q].transpose(1, 0, 2).reshape(n_q * N_HEADS, -1)

    page_t = lambda c: c.transpose(0, 1, 3, 4, 2).reshape(c.shape[0], c.shape[1], ATT_WIDTH, PAGE_SIZE)
    cache_kt, cache_vt = page_t(cache_k), page_t(cache_v)

    time_major = lambda a: a.transpose(1, 0, 2).reshape(-1, a.shape[2])
    seq_major = lambda a, steps: a.reshape(steps, n_seq, -1).transpose(1, 0, 2)
    rows_s = n_q * n_seq

    xp = x_prompt
    xs = time_major(x_sample)[None]
    fin = final_norm_g.reshape(1, D_MODEL)
    outs = [[] for _ in range(8)]
    for l in range(depth):
        g_mix = norm_mix_g[l].reshape(1, D_MODEL)
        g_ffn = norm_ffn_g[l].reshape(1, D_MODEL)
        w_in_bf = w_in[l].astype(_BF)
        wa, wc, wo = w_att_out[l].astype(_BF), w_conv_out[l].astype(_BF), w_o[l].astype(_BF)
        wu, wd = w_up[l].astype(_BF), w_down[l].astype(_BF)
        cbias = ffn_conv_b[l].reshape(1, -1)
        final_g = fin if l == depth - 1 else None

        k, v, kbf, qtz, vt, kmean, cb, ga, gb, cstate = _inproj_prompt(xp, mod_p[l], g_mix, w_in_bf, conv_w[l])
        att_t = _attn_prompt(qtz, kbf, vt, kmean, bias_p)
        x1, h2 = _post(xp, att_t, cb, ga, gb, mod_p[l], g_ffn, wa, wc, wo, tr=BLOCK, att_transposed=True)
        xp, fstate = _ffn(x1, h2, mod_p[l], wu, ffn_conv_w[l], cbias, wd, None, final_g, tr=BLOCK, stride=1)
        outs[0].append(k.reshape(bsz, seq, N_HEADS, HEAD_DIM))
        outs[1].append(v.reshape(bsz, seq, N_HEADS, HEAD_DIM))
        outs[2].append(cstate)
        outs[3].append(fstate)

        q_s, k_s, v_s, cb_s, ga_s, gb_s, cstate_s = _inproj_sample(
            xs[0], mod_s[l], g_mix, w_in_bf, conv_w[l], time_major(state_conv[l]), n_q)
        k_sm, v_sm = seq_major(k_s, n_q), seq_major(v_s, n_q)
        pad = ((0, 0), (0, 8 - n_q), (0, 0))
        att_s = _attn_sample(page_table, seq_major(q_s, n_q), jnp.pad(k_sm, pad), jnp.pad(v_sm, pad),
                             bias_s, cache_kt, cache_vt, l)
        x1s, h2s = _post(xs, time_major(att_s)[None], cb_s[None], ga_s[None], gb_s[None], mod_s[l], g_ffn,
                         wa, wc, wo, tr=rows_s, att_transposed=False)
        xs, fstate_s = _ffn(x1s, h2s, mod_s[l], wu, ffn_conv_w[l], cbias, wd, time_major(state_ffn[l]),
                            final_g, tr=rows_s, stride=n_seq)
        outs[4].append(k_sm.reshape(n_seq, n_q, N_HEADS, HEAD_DIM))
        outs[5].append(v_sm.reshape(n_seq, n_q, N_HEADS, HEAD_DIM))
        outs[6].append(seq_major(cstate_s, 2))
        outs[7].append(seq_major(fstate_s, 2))

    y_sample = seq_major(xs[0], n_q)
    return (xp, y_sample) + tuple(jnp.stack(o) for o in outs)
```

```python
import functools

import jax
import jax.numpy as jnp
from jax import lax
from jax.experimental import pallas as pl
from jax.experimental.pallas import tpu as pltpu

D_MODEL = 1024
N_HEADS = 8
HEAD_DIM = 64
ATT_WIDTH = N_HEADS * HEAD_DIM
CONV_WIDTH = 512
BLOCK = 256
TOP_K = 3
N_BUCKETS = 32
MAX_DIST = 1024
D_FF = 2816
PAGE_SIZE = 128
EPS = 1e-6
NEG = -1e30
REMOVED = -3e38
LOG2E = 1.4426950408889634
N_IN = 3 * ATT_WIDTH + 3 * CONV_WIDTH + 2 * D_MODEL
LANES = 128
FFN_CHUNK = 256
N_BIAS_TILES = 6
MAX_BLOCKS = 32
V_ROWS = HEAD_DIM + 16
VMEM_LIMIT = 56 * 1024 * 1024

_BF = jnp.bfloat16
_F32 = jnp.float32


def _bucket_upper_bounds():
    max_exact = N_BUCKETS // 2
    n_log = N_BUCKETS - max_exact
    ratio = MAX_DIST // max_exact
    assert ratio * max_exact == MAX_DIST
    ups = [b + 1 for b in range(max_exact)]
    for k in range(1, n_log):
        target = (max_exact ** n_log) * (ratio ** k)
        n = max_exact
        while n ** n_log < target:
            n += 1
        ups.append(n)
    return ups


_BUCKET_UPPER = _bucket_upper_bounds()
assert (N_BIAS_TILES - 2) * BLOCK + 1 >= _BUCKET_UPPER[-1]


def _dot(a, b):
    return jnp.dot(a, b, preferred_element_type=_F32)


def _dot_nt(a, b):
    return lax.dot_general(a, b, (((1,), (1,)), ((), ())), preferred_element_type=_F32)


def _sigmoid(x):
    return 1.0 / (1.0 + jnp.exp(-x))


def _silu(x):
    return x * _sigmoid(x)


def _rmsnorm(x, g):
    return x * lax.rsqrt(jnp.mean(x * x, axis=-1, keepdims=True) + EPS) * g


def _mod_chunk(mod_ref, idx, reps):
    v = mod_ref[0, :, idx * D_MODEL:(idx + 1) * D_MODEL]
    if reps > 1:
        v = jnp.concatenate([v] * reps, axis=0)
    return v


def _params(sem, vmem=VMEM_LIMIT):
    return pltpu.CompilerParams(dimension_semantics=sem, vmem_limit_bytes=vmem)


def _const_spec(shape):
    nd = len(shape)
    return pl.BlockSpec(shape, lambda *_: (0,) * nd, pipeline_mode=pl.Buffered(1))


def _mod_kernel(c_ref, w_ref, b_ref, o_ref):
    a = _silu(c_ref[...]).astype(_BF)
    o_ref[0] = _dot(a, w_ref[0].astype(_BF)) + b_ref[0]


def _modulation(c_all, ada_w, ada_b):
    depth = ada_w.shape[0]
    rows = c_all.shape[0]
    n_chunks = ada_w.shape[2] // D_MODEL
    return pl.pallas_call(
        _mod_kernel,
        grid=(depth, n_chunks),
        in_specs=[
            pl.BlockSpec((rows, D_MODEL), lambda l, n: (0, 0)),
            pl.BlockSpec((1, D_MODEL, D_MODEL), lambda l, n: (l, 0, n)),
            pl.BlockSpec((1, 1, D_MODEL), lambda l, n: (l, 0, n)),
        ],
        out_specs=pl.BlockSpec((1, rows, D_MODEL), lambda l, n: (l, 0, n)),
        out_shape=jax.ShapeDtypeStruct((depth, rows, ada_w.shape[2]), _F32),
        compiler_params=_params(("arbitrary", "arbitrary")),
        name="modulation",
    )(c_all, ada_w, ada_b.reshape(depth, 1, -1))


def _bias_kernel(rb_ref, bp_ref, bs_ref, *, past_len):
    h = pl.program_id(0)

    def table(dist, scale):
        n = jnp.maximum(dist, 0)
        v = jnp.full(dist.shape, rb_ref[N_BUCKETS - 1, h], _F32)
        for b in range(N_BUCKETS - 2, -1, -1):
            v = jnp.where(n < _BUCKET_UPPER[b], rb_ref[b, h], v)
        return jnp.where(dist >= 0, v * scale, NEG)

    kk = lax.broadcasted_iota(jnp.int32, (BLOCK, BLOCK), 0)
    qq = lax.broadcasted_iota(jnp.int32, (BLOCK, BLOCK), 1)
    for d in range(N_BIAS_TILES):
        bp_ref[0, d] = table(d * BLOCK + qq - kk, LOG2E)
    t = lax.broadcasted_iota(jnp.int32, bs_ref.shape[1:], 0)
    p = lax.broadcasted_iota(jnp.int32, bs_ref.shape[1:], 1)
    bs_ref[0] = table(past_len + t - p, 1.0)


def _bias_tables(rel_bias, past_len):
    s_cols = past_len + PAGE_SIZE
    return pl.pallas_call(
        functools.partial(_bias_kernel, past_len=past_len),
        grid=(N_HEADS,),
        in_specs=[pl.BlockSpec(memory_space=pltpu.SMEM)],
        out_specs=[
            pl.BlockSpec((1, N_BIAS_TILES, BLOCK, BLOCK), lambda h: (h, 0, 0, 0)),
            pl.BlockSpec((1, 8, s_cols), lambda h: (h, 0, 0)),
        ],
        out_shape=[
            jax.ShapeDtypeStruct((N_HEADS, N_BIAS_TILES, BLOCK, BLOCK), _F32),
            jax.ShapeDtypeStruct((N_HEADS, 8, s_cols), _F32),
        ],
        compiler_params=_params(("arbitrary",)),
        name="bias_tables",
    )(rel_bias)


def _modulated_norm(x, g, sh, sc):
    return (_rmsnorm(x, g) * (1.0 + sc) + sh).astype(_BF)


def _project(hb, w_ref):
    bounds = (0, ATT_WIDTH, 2 * ATT_WIDTH, 3 * ATT_WIDTH, 3 * ATT_WIDTH + CONV_WIDTH,
              3 * ATT_WIDTH + 2 * CONV_WIDTH, 3 * ATT_WIDTH + 3 * CONV_WIDTH,
              3 * ATT_WIDTH + 3 * CONV_WIDTH + D_MODEL, N_IN)
    return [_dot(hb, w_ref[:, a:b]) for a, b in zip(bounds[:-1], bounds[1:])]


def _inproj_prompt_kernel(x_ref, mod_ref, g_ref, w_ref, wkx_ref, cw_ref,
                          k_ref, v_ref, kx_ref, qtz_ref, qt_ref, vt_ref, kmean_ref, cb_ref, ga_ref, gb_ref,
                          cstate_ref, ext_ref):
    tr = x_ref.shape[1]
    t = pl.program_id(1)

    @pl.when(t == 0)
    def _():
        ext_ref[0:8, :] = jnp.zeros((8, CONV_WIDTH), _F32)

    hb = _modulated_norm(x_ref[0], g_ref[...], _mod_chunk(mod_ref, 0, 1), _mod_chunk(mod_ref, 1, 1))
    q, k, v, u, bg, cg, ga, gb = _project(hb, w_ref)

    qt = jnp.transpose(q * (HEAD_DIM ** -0.5))
    zeros = jnp.zeros((HEAD_DIM, tr), _F32)
    for h in range(N_HEADS):
        qh = qt[h * HEAD_DIM:(h + 1) * HEAD_DIM]
        pair = [qh, zeros] if h % 2 == 0 else [zeros, qh]
        qtz_ref[0, h] = jnp.concatenate(pair, axis=0).astype(_BF)
    qt_ref[0] = (qt * LOG2E).astype(_BF)

    k_ref[0] = k
    kmean_ref[0, 0] = jnp.mean(k, axis=0, keepdims=True)
    lane = lax.broadcasted_iota(jnp.int32, (tr, N_HEADS * LANES), 1) % LANES
    block_lanes = (lane == HEAD_DIM + t) | (lane == HEAD_DIM + MAX_BLOCKS + t)
    kx = jnp.where(block_lanes, 1.0, _dot(hb, wkx_ref[...])).astype(_BF)
    for h in range(N_HEADS):
        kx_ref[0, 0, h] = kx[:, h * LANES:(h + 1) * LANES]

    v_ref[0] = v
    vt = jnp.transpose(v)
    ones = jnp.ones((V_ROWS - HEAD_DIM, tr), _BF)
    for h in range(N_HEADS):
        vt_ref[0, h, 0] = jnp.concatenate([vt[h * HEAD_DIM:(h + 1) * HEAD_DIM].astype(_BF), ones], axis=0)

    ext_ref[8:8 + tr, :] = cg * u
    cw = cw_ref[...]
    conv = (cw[0:1] * ext_ref[6:6 + tr, :] + cw[1:2] * ext_ref[7:7 + tr, :]
            + cw[2:3] * ext_ref[8:8 + tr, :])
    cb_ref[0] = (bg * conv).astype(_BF)
    cstate_ref[0] = ext_ref[tr + 6:tr + 8, :]
    ext_ref[0:8, :] = ext_ref[tr:tr + 8, :]

    ga_ref[0] = _sigmoid(ga)
    gb_ref[0] = _sigmoid(gb)


def _inproj_prompt(x, mod, g, w_bf, wkx_bf, cw):
    bsz, seq, _ = x.shape
    tr = BLOCK
    nt = seq // tr
    assert nt == MAX_BLOCKS
    row = lambda w: pl.BlockSpec((1, tr, w), lambda b, t: (b, t, 0))
    return pl.pallas_call(
        _inproj_prompt_kernel,
        grid=(bsz, nt),
        in_specs=[
            row(D_MODEL),
            pl.BlockSpec((1, 1, mod.shape[2]), lambda b, t: (b, 0, 0)),
            _const_spec((1, D_MODEL)),
            _const_spec((D_MODEL, N_IN)),
            _const_spec((D_MODEL, N_HEADS * LANES)),
            _const_spec((3, CONV_WIDTH)),
        ],
        out_specs=[
            row(ATT_WIDTH), row(ATT_WIDTH),
            pl.BlockSpec((1, 1, N_HEADS, tr, LANES), lambda b, t: (b, t, 0, 0, 0)),
            pl.BlockSpec((1, N_HEADS, 2 * HEAD_DIM, tr), lambda b, t: (b, 0, 0, t)),
            pl.BlockSpec((1, ATT_WIDTH, tr), lambda b, t: (b, 0, t)),
            pl.BlockSpec((1, N_HEADS, 1, V_ROWS, tr), lambda b, t: (b, 0, t, 0, 0)),
            pl.BlockSpec((1, 1, 1, ATT_WIDTH), lambda b, t: (b, t, 0, 0)),
            row(CONV_WIDTH), row(D_MODEL), row(D_MODEL),
            pl.BlockSpec((1, 2, CONV_WIDTH), lambda b, t: (b, 0, 0)),
        ],
        out_shape=[
            jax.ShapeDtypeStruct((bsz, seq, ATT_WIDTH), _F32),
            jax.ShapeDtypeStruct((bsz, seq, ATT_WIDTH), _F32),
            jax.ShapeDtypeStruct((bsz, nt, N_HEADS, tr, LANES), _BF),
            jax.ShapeDtypeStruct((bsz, N_HEADS, 2 * HEAD_DIM, seq), _BF),
            jax.ShapeDtypeStruct((bsz, ATT_WIDTH, seq), _BF),
            jax.ShapeDtypeStruct((bsz, N_HEADS, nt, V_ROWS, tr), _BF),
            jax.ShapeDtypeStruct((bsz, nt, 1, ATT_WIDTH), _F32),
            jax.ShapeDtypeStruct((bsz, seq, CONV_WIDTH), _BF),
            jax.ShapeDtypeStruct((bsz, seq, D_MODEL), _F32),
            jax.ShapeDtypeStruct((bsz, seq, D_MODEL), _F32),
            jax.ShapeDtypeStruct((bsz, 2, CONV_WIDTH), _F32),
        ],
        scratch_shapes=[pltpu.VMEM((tr + 8, CONV_WIDTH), _F32)],
        compiler_params=_params(("arbitrary", "arbitrary")),
        name="inproj_prompt",
    )(x, mod, g, w_bf, wkx_bf, cw)


def _inproj_sample_kernel(x_ref, mod_ref, g_ref, w_ref, cw_ref, prev_ref,
                          q_ref, k_ref, v_ref, cb_ref, ga_ref, gb_ref, cstate_ref, *, reps):
    rows = x_ref.shape[0]
    stride = rows // reps
    hb = _modulated_norm(x_ref[...], g_ref[...], _mod_chunk(mod_ref, 0, reps), _mod_chunk(mod_ref, 1, reps))
    q, k, v, u, bg, cg, ga, gb = _project(hb, w_ref)
    q_ref[...] = q * (HEAD_DIM ** -0.5)
    k_ref[...] = k
    v_ref[...] = v
    cu = cg * u
    ext = jnp.concatenate([prev_ref[...], cu], axis=0)
    cw = cw_ref[...]
    conv = (cw[0:1] * ext[0:rows] + cw[1:2] * ext[stride:stride + rows]
            + cw[2:3] * ext[2 * stride:2 * stride + rows])
    cb_ref[...] = (bg * conv).astype(_BF)
    cstate_ref[...] = ext[rows:rows + 2 * stride]
    ga_ref[...] = _sigmoid(ga)
    gb_ref[...] = _sigmoid(gb)


def _inproj_sample(x, mod, g, w_bf, cw, prev, reps):
    rows = x.shape[0]
    full = lambda a: pl.BlockSpec(a.shape, lambda i: (0,) * a.ndim)
    shp = lambda w, dt=_F32: jax.ShapeDtypeStruct((rows, w), dt)
    outs = [shp(ATT_WIDTH), shp(ATT_WIDTH), shp(ATT_WIDTH), shp(CONV_WIDTH, _BF),
            shp(D_MODEL), shp(D_MODEL), jax.ShapeDtypeStruct(prev.shape, _F32)]
    return pl.pallas_call(
        functools.partial(_inproj_sample_kernel, reps=reps),
        grid=(1,),
        in_specs=[full(x), full(mod), full(g), full(w_bf), full(cw), full(prev)],
        out_specs=[pl.BlockSpec(o.shape, lambda i: (0, 0)) for o in outs],
        out_shape=outs,
        compiler_params=_params(("arbitrary",)),
        name="inproj_sample",
    )(x, mod, g, w_bf, cw, prev)


def _top_k_rows(gate, own):
    n = gate.shape[0]
    row = lax.broadcasted_iota(jnp.int32, gate.shape, 0).astype(_F32)
    past = row < own
    gate = jnp.where(past, gate, NEG)
    mask = jnp.where(row == own, 0.0, NEG)
    for _ in range(min(TOP_K, n)):
        mx = jnp.max(gate, axis=0, keepdims=True)
        first = jnp.min(jnp.where(gate == mx, row, float(n)), axis=0, keepdims=True)
        chosen = row == first
        mask = jnp.where(chosen, jnp.where(past, 0.0, mask), mask)
        gate = jnp.where(chosen, REMOVED, gate)
    return mask


def _attn_prompt_kernel(qtz_ref, qt_ref, kx_ref, vt_ref, kmean_ref, bias_ref, o_ref,
                        qx_ref, m_ref, acc_ref, s_ref):
    i = pl.program_id(1)
    tq = qt_ref.shape[2]
    own = i.astype(_F32)
    n_far = jnp.maximum(i - (N_BIAS_TILES - 2), 0)
    is_far = lax.broadcasted_iota(jnp.int32, (MAX_BLOCKS, tq), 0) < n_far
    for h in range(N_HEADS):
        km = kmean_ref[0, :, (h // 2) * LANES:(h // 2 + 1) * LANES].astype(_BF)
        mask = _top_k_rows(_dot(km, qtz_ref[0, h]), own)
        far_bias = bias_ref[h, N_BIAS_TILES - 1, 0:1, 0:1]
        ext = mask + jnp.where(is_far, far_bias, 0.0)
        hi = ext.astype(_BF)
        qx_ref[h, 0:HEAD_DIM] = qt_ref[0, h * HEAD_DIM:(h + 1) * HEAD_DIM, :]
        qx_ref[h, HEAD_DIM:HEAD_DIM + MAX_BLOCKS] = hi
        qx_ref[h, HEAD_DIM + MAX_BLOCKS:] = (ext - hi.astype(_F32)).astype(_BF)
    m_ref[...] = jnp.full(m_ref.shape, NEG, _F32)
    acc_ref[...] = jnp.zeros(acc_ref.shape, _F32)

    n_pairs = N_HEADS // 2
    last_block = kx_ref.shape[1] - 1

    def logits(j, pair):
        return jnp.concatenate(
            [_dot(kx_ref[0, j, 2 * pair + e], qx_ref[2 * pair + e]) for e in range(2)], axis=1)

    ahead = s_ref.shape[0]
    for pair in range(ahead):
        s_ref[pair] = logits(0, pair)

    def body(j, carry, near):
        pending = [s_ref[pair] for pair in range(ahead)]
        for pair in range(n_pairs):
            s2 = pending.pop(0)
            if pair + ahead < n_pairs:
                pending.append(logits(j, pair + ahead))
            else:
                s_ref[pair + ahead - n_pairs] = logits(jnp.minimum(j + 1, last_block), pair + ahead - n_pairs)
            for e in range(2):
                h = 2 * pair + e
                s = s2[:, e * tq:(e + 1) * tq]
                if near:
                    s = s + bias_ref[h, i - j]
                m_old = m_ref[h:h + 1, :]
                m_new = jnp.maximum(m_old, jnp.max(s, axis=0, keepdims=True))
                alpha = jnp.exp2(m_old - m_new)
                p = jnp.exp2(s - m_new)
                m_ref[h:h + 1, :] = m_new
                acc_ref[h] = alpha * acc_ref[h] + _dot(vt_ref[0, h, j], p.astype(_BF))
        return carry

    def run(lo, hi, near):
        def two_blocks(jj, carry):
            body(lo + 2 * jj, carry, near)
            return body(lo + 2 * jj + 1, carry, near)

        lax.fori_loop(0, (hi - lo) // 2, two_blocks, 0)

        @pl.when((hi - lo) % 2 == 1)
        def _():
            body(hi - 1, 0, near)

    run(0, n_far, False)
    run(n_far, i + 1, True)
    for h in range(N_HEADS):
        acc = acc_ref[h]
        o_ref[0, h * HEAD_DIM:(h + 1) * HEAD_DIM, :] = (acc[:HEAD_DIM] / acc[HEAD_DIM:HEAD_DIM + 1]).astype(_BF)


def _attn_prompt(qtz, qt, kx, vt, kmean, bias_p):
    bsz, _, seq = qt.shape
    nb = seq // BLOCK
    km3 = kmean.reshape(bsz, nb, ATT_WIDTH)
    per_batch = lambda shape: pl.BlockSpec(
        (1,) + shape[1:], lambda b, i: (b,) + (0,) * (len(shape) - 1), pipeline_mode=pl.Buffered(1))
    return pl.pallas_call(
        _attn_prompt_kernel,
        grid=(bsz, nb),
        in_specs=[
            pl.BlockSpec((1, N_HEADS, 2 * HEAD_DIM, BLOCK), lambda b, i: (b, 0, 0, i)),
            pl.BlockSpec((1, ATT_WIDTH, BLOCK), lambda b, i: (b, 0, i)),
            per_batch(kx.shape), per_batch(vt.shape), per_batch(km3.shape),
            _const_spec(bias_p.shape),
        ],
        out_specs=pl.BlockSpec((1, ATT_WIDTH, BLOCK), lambda b, i: (b, 0, i)),
        out_shape=jax.ShapeDtypeStruct((bsz, ATT_WIDTH, seq), _BF),
        scratch_shapes=[
            pltpu.VMEM((N_HEADS, LANES, BLOCK), _BF),
            pltpu.VMEM((N_HEADS, BLOCK), _F32),
            pltpu.VMEM((N_HEADS, V_ROWS, BLOCK), _F32),
            pltpu.VMEM((2, BLOCK, 2 * BLOCK), _F32),
        ],
        compiler_params=_params(("arbitrary", "arbitrary")),
        name="attn_prompt",
    )(qtz, qt, kx, vt, km3, bias_p)


def _attn_sample_kernel(pt_ref, q_ref, kn_ref, vn_ref, bias_ref, *refs, n_pages, n_q):
    del pt_ref
    k_refs, v_refs, o_ref = refs[:n_pages], refs[n_pages:2 * n_pages], refs[2 * n_pages]
    rows = n_q * N_HEADS
    pages_per_block = BLOCK // PAGE_SIZE
    n_blocks = n_pages // pages_per_block

    q = q_ref[0]
    rep = jnp.concatenate([jnp.broadcast_to(q[t:t + 1], (N_HEADS, ATT_WIDTH)) for t in range(n_q)], axis=0)
    r_head = lax.broadcasted_iota(jnp.int32, (rows, ATT_WIDTH), 0) % N_HEADS
    c_head = lax.broadcasted_iota(jnp.int32, (rows, ATT_WIDTH), 1) // HEAD_DIM
    head_lanes = r_head == c_head
    q_rows = jnp.where(head_lanes, rep, 0.0).astype(_BF)

    kt_pages = [r[0, 0] for r in k_refs]
    block_sums = [
        jnp.sum(sum(kt_pages[n * pages_per_block:(n + 1) * pages_per_block]), axis=1, keepdims=True)
        for n in range(n_blocks)]
    kmean_t = jnp.concatenate(block_sums, axis=1) * (1.0 / BLOCK)
    gate = _dot(q_rows, kmean_t.astype(_BF))

    col = lax.broadcasted_iota(jnp.int32, gate.shape, 1).astype(_F32)
    sel = jnp.full(gate.shape, NEG, _F32)
    for _ in range(min(TOP_K, n_blocks)):
        mx = jnp.max(gate, axis=1, keepdims=True)
        first = jnp.min(jnp.where(gate == mx, col, float(n_blocks)), axis=1, keepdims=True)
        chosen = col == first
        sel = jnp.where(chosen, 0.0, sel)
        gate = jnp.where(chosen, REMOVED, gate)

    logits = []
    for p in range(n_pages):
        s = _dot(q_rows, kt_pages[p].astype(_BF)) + bias_ref[:, p * PAGE_SIZE:(p + 1) * PAGE_SIZE]
        n = p // pages_per_block
        logits.append(s + sel[:, n:n + 1])
    pad = jnp.zeros((PAGE_SIZE - kn_ref.shape[1], ATT_WIDTH), _F32)
    k_own = jnp.concatenate([kn_ref[0], pad], axis=0).astype(_BF)
    v_own = jnp.concatenate([vn_ref[0], pad], axis=0).astype(_BF)
    logits.append(_dot_nt(q_rows, k_own) + bias_ref[:, n_pages * PAGE_SIZE:])

    m = functools.reduce(jnp.maximum, logits)
    m = jnp.max(m, axis=1, keepdims=True)
    l = jnp.zeros((rows, 1), _F32)
    acc = jnp.zeros((rows, ATT_WIDTH), _F32)
    for p in range(n_pages + 1):
        pr = jnp.exp(logits[p] - m)
        l = l + jnp.sum(pr, axis=1, keepdims=True)
        if p == n_pages:
            acc = acc + _dot(pr.astype(_BF), v_own)
        else:
            acc = acc + _dot_nt(pr.astype(_BF), v_refs[p][0, 0].astype(_BF))
    out = jnp.where(head_lanes, acc / l, 0.0)
    o_ref[0] = jnp.sum(out.reshape(n_q, N_HEADS, ATT_WIDTH), axis=1)


def _attn_sample(page_table, q, k_new, v_new, bias_s, cache_kt, cache_vt, layer):
    n_seq, n_pages = page_table.shape
    n_q = q.shape[1]
    page_spec = lambda p: pl.BlockSpec(
        (1, 1, ATT_WIDTH, PAGE_SIZE), lambda b, pt, p=p: (layer, pt[b, p], 0, 0))
    per_seq = lambda a: pl.BlockSpec((1,) + a.shape[1:], lambda b, pt: (b, 0, 0))
    grid_spec = pltpu.PrefetchScalarGridSpec(
        num_scalar_prefetch=1,
        grid=(n_seq,),
        in_specs=[per_seq(q), per_seq(k_new), per_seq(v_new),
                  pl.BlockSpec(bias_s.shape, lambda b, pt: (0, 0), pipeline_mode=pl.Buffered(1))]
                 + [page_spec(p) for p in range(n_pages)] * 2,
        out_specs=pl.BlockSpec((1, n_q, ATT_WIDTH), lambda b, pt: (b, 0, 0)),
    )
    return pl.pallas_call(
        functools.partial(_attn_sample_kernel, n_pages=n_pages, n_q=n_q),
        grid_spec=grid_spec,
        out_shape=jax.ShapeDtypeStruct((n_seq, n_q, ATT_WIDTH), _F32),
        compiler_params=_params(("arbitrary",)),
        name="attn_sample",
    )(page_table, q, k_new, v_new, bias_s, *([cache_kt] * n_pages), *([cache_vt] * n_pages))


def _post_kernel(x_ref, att_ref, cb_ref, ga_ref, gb_ref, mod_ref, g_ref, wa_ref, wc_ref, wo_ref,
                 x1_ref, h2_ref, *, reps, att_transposed):
    if att_transposed:
        att = jnp.transpose(att_ref[0].astype(_F32)).astype(_BF)
    else:
        att = att_ref[0].astype(_BF)
    y_a = _dot(att, wa_ref[...])
    y_b = _dot(cb_ref[0], wc_ref[...])
    merged = ga_ref[0] * y_a + gb_ref[0] * y_b
    x1 = x_ref[0] + _mod_chunk(mod_ref, 2, reps) * _dot(merged.astype(_BF), wo_ref[...])
    x1_ref[0] = x1
    h2 = _rmsnorm(x1, g_ref[...]) * (1.0 + _mod_chunk(mod_ref, 4, reps)) + _mod_chunk(mod_ref, 3, reps)
    h2_ref[0] = h2.astype(_BF)


def _post(x, att, cb, ga, gb, mod, g, wa, wc, wo, *, tr, att_transposed):
    bsz, seq, _ = x.shape
    nt = seq // tr
    reps = 1 if mod.shape[1] == 1 else tr // mod.shape[1]
    row = lambda w: pl.BlockSpec((1, tr, w), lambda b, t: (b, t, 0))
    att_spec = (pl.BlockSpec((1, ATT_WIDTH, tr), lambda b, t: (b, 0, t)) if att_transposed
                else row(ATT_WIDTH))
    return pl.pallas_call(
        functools.partial(_post_kernel, reps=reps, att_transposed=att_transposed),
        grid=(bsz, nt),
        in_specs=[
            row(D_MODEL), att_spec, row(CONV_WIDTH), row(D_MODEL), row(D_MODEL),
            pl.BlockSpec((1,) + mod.shape[1:], lambda b, t: (b, 0, 0)),
            _const_spec((1, D_MODEL)),
            _const_spec(wa.shape), _const_spec(wc.shape), _const_spec(wo.shape),
        ],
        out_specs=[row(D_MODEL), row(D_MODEL)],
        out_shape=[jax.ShapeDtypeStruct(x.shape, _F32), jax.ShapeDtypeStruct(x.shape, _BF)],
        compiler_params=_params(("arbitrary", "arbitrary")),
        name="post_attn",
    )(x, att, cb, ga, gb, mod, g, wa, wc, wo)


def _ffn_kernel(*refs, reps, stride, final):
    if stride == 1:
        x1_ref, h2_ref, mod_ref, wu_ref, cw_ref, cbias_ref, wd_ref = refs[:7]
        rest = refs[7:]
        prev_ref = None
    else:
        x1_ref, h2_ref, mod_ref, wu_ref, cw_ref, cbias_ref, wd_ref, prev_ref = refs[:8]
        rest = refs[8:]
    if final:
        fg_ref, rest = rest[0], rest[1:]
    out_ref, state_ref = rest[0], rest[1]
    ext_ref = rest[2] if stride == 1 else None

    rows = x1_ref.shape[1]
    hb = h2_ref[0]
    cw = cw_ref[...]
    cbias = cbias_ref[...]

    if stride == 1:
        @pl.when(pl.program_id(1) == 0)
        def _():
            ext_ref[0:8, :] = jnp.zeros((8, 2 * D_FF), _F32)

    n_chunks = D_FF // FFN_CHUNK
    a_cols = lambda c: slice(c * FFN_CHUNK, (c + 1) * FFN_CHUNK)
    b_cols = lambda c: slice(D_FF + c * FFN_CHUNK, D_FF + (c + 1) * FFN_CHUNK)
    up_pair = lambda c: (_dot(hb, wu_ref[:, a_cols(c)]), _dot(hb, wu_ref[:, b_cols(c)]))

    def conv_cols(up, sl):
        if stride == 1:
            ext_ref[8:8 + rows, sl] = up
            e0, e1, e2 = ext_ref[6:6 + rows, sl], ext_ref[7:7 + rows, sl], up
        else:
            ext = jnp.concatenate([prev_ref[:, sl], up], axis=0)
            e0, e1, e2 = ext[0:rows], ext[stride:stride + rows], up
            state_ref[:, sl] = ext[rows:rows + 2 * stride]
        return cw[0:1, sl] * e0 + cw[1:2, sl] * e1 + cw[2:3, sl] * e2 + cbias[:, sl]

    acc = jnp.zeros((rows, D_MODEL), _F32)
    up_next = up_pair(0)
    for c in range(n_chunks):
        up_a, up_b = up_next
        if c + 1 < n_chunks:
            up_next = up_pair(c + 1)
        a = conv_cols(up_a, a_cols(c))
        b = conv_cols(up_b, b_cols(c))
        acc = acc + _dot((_silu(a) * b).astype(_BF), wd_ref[a_cols(c), :])

    if stride == 1:
        state_ref[0] = ext_ref[rows + 6:rows + 8, :]
        ext_ref[0:8, :] = ext_ref[rows:rows + 8, :]

    x2 = x1_ref[0] + _mod_chunk(mod_ref, 5, reps) * acc
    out_ref[0] = _rmsnorm(x2, fg_ref[...]) if final else x2


def _ffn(x1, h2, mod, wu, cw, cbias, wd, prev, final_g, *, tr, stride):
    bsz, seq, _ = x1.shape
    nt = seq // tr
    reps = 1 if mod.shape[1] == 1 else tr // mod.shape[1]
    final = final_g is not None
    row = lambda w: pl.BlockSpec((1, tr, w), lambda b, t: (b, t, 0))
    in_specs = [row(D_MODEL), row(D_MODEL),
                pl.BlockSpec((1,) + mod.shape[1:], lambda b, t: (b, 0, 0)),
                _const_spec(wu.shape), _const_spec(cw.shape), _const_spec(cbias.shape), _const_spec(wd.shape)]
    args = [x1, h2, mod, wu, cw, cbias, wd]
    if stride == 1:
        state_shape = jax.ShapeDtypeStruct((bsz, 2, 2 * D_FF), _F32)
        state_spec = pl.BlockSpec((1, 2, 2 * D_FF), lambda b, t: (b, 0, 0))
        scratch = [pltpu.VMEM((tr + 8, 2 * D_FF), _F32)]
    else:
        assert bsz == 1 and nt == 1 and tr % stride == 0
        in_specs.append(_const_spec(prev.shape))
        args.append(prev)
        state_shape = jax.ShapeDtypeStruct(prev.shape, _F32)
        state_spec = pl.BlockSpec(prev.shape, lambda b, t: (0, 0))
        scratch = []
    if final:
        in_specs.append(_const_spec(final_g.shape))
        args.append(final_g)
    return pl.pallas_call(
        functools.partial(_ffn_kernel, reps=reps, stride=stride, final=final),
        grid=(bsz, nt),
        in_specs=in_specs,
        out_specs=[row(D_MODEL), state_spec],
        out_shape=[jax.ShapeDtypeStruct(x1.shape, _F32), state_shape],
        scratch_shapes=scratch,
        compiler_params=_params(("arbitrary", "arbitrary")),
        name="conv_ffn",
    )(*args)


def kernel(x_prompt, x_sample, c_prompt, c_sample, cache_k, cache_v, state_conv, state_ffn, page_table,
           rel_bias, ada_w, ada_b, norm_mix_g, norm_ffn_g, final_norm_g, w_in, conv_w, w_att_out,
           w_conv_out, w_o, w_up, ffn_conv_w, ffn_conv_b, w_down):
    depth = w_in.shape[0]
    bsz, seq, _ = x_prompt.shape
    n_seq, n_q, _ = x_sample.shape
    n_pages = page_table.shape[1]
    past_len = n_pages * PAGE_SIZE
    assert seq % BLOCK == 0 and past_len % BLOCK == 0 and n_q <= 8 and bsz <= 8
    assert cache_k.shape[2] == PAGE_SIZE

    pad_rows = 8
    c_all = jnp.concatenate([c_prompt, jnp.zeros((pad_rows - bsz, D_MODEL), _F32), c_sample], axis=0)
    mod_all = _modulation(c_all, ada_w, ada_b)
    mod_p = mod_all[:, :bsz].reshape(depth, bsz, 1, -1)
    mod_s = mod_all[:, pad_rows:].reshape(depth, 1, n_seq, -1)

    bias_p, bias_s8 = _bias_tables(rel_bias, past_len)
    bias_s = bias_s8[:, :n_q].transpose(1, 0, 2).reshape(n_q * N_HEADS, -1)

    page_t = lambda c: c.transpose(0, 1, 3, 4, 2).reshape(c.shape[0], c.shape[1], ATT_WIDTH, PAGE_SIZE)
    cache_kt, cache_vt = page_t(cache_k), page_t(cache_v)

    time_major = lambda a: a.transpose(1, 0, 2).reshape(-1, a.shape[2])
    seq_major = lambda a, steps: a.reshape(steps, n_seq, -1).transpose(1, 0, 2)
    rows_s = n_q * n_seq

    xp = x_prompt
    xs = time_major(x_sample)[None]
    fin = final_norm_g.reshape(1, D_MODEL)
    outs = [[] for _ in range(8)]
    for l in range(depth):
        g_mix = norm_mix_g[l].reshape(1, D_MODEL)
        g_ffn = norm_ffn_g[l].reshape(1, D_MODEL)
        w_in_bf = w_in[l].astype(_BF)
        wk = w_in[l][:, ATT_WIDTH:2 * ATT_WIDTH].reshape(D_MODEL, N_HEADS, HEAD_DIM)
        wkx_bf = jnp.pad(wk, ((0, 0), (0, 0), (0, LANES - HEAD_DIM))).reshape(D_MODEL, N_HEADS * LANES).astype(_BF)
        wa, wc, wo = w_att_out[l].astype(_BF), w_conv_out[l].astype(_BF), w_o[l].astype(_BF)
        wu, wd = w_up[l].astype(_BF), w_down[l].astype(_BF)
        cbias = ffn_conv_b[l].reshape(1, -1)
        final_g = fin if l == depth - 1 else None

        k, v, kx, qtz, qt, vt, kmean, cb, ga, gb, cstate = _inproj_prompt(
            xp, mod_p[l], g_mix, w_in_bf, wkx_bf, conv_w[l])
        att_t = _attn_prompt(qtz, qt, kx, vt, kmean, bias_p)
        x1, h2 = _post(xp, att_t, cb, ga, gb, mod_p[l], g_ffn, wa, wc, wo, tr=BLOCK, att_transposed=True)
        xp, fstate = _ffn(x1, h2, mod_p[l], wu, ffn_conv_w[l], cbias, wd, None, final_g, tr=BLOCK, stride=1)
        outs[0].append(k.reshape(bsz, seq, N_HEADS, HEAD_DIM))
        outs[1].append(v.reshape(bsz, seq, N_HEADS, HEAD_DIM))
        outs[2].append(cstate)
        outs[3].append(fstate)

        q_s, k_s, v_s, cb_s, ga_s, gb_s, cstate_s = _inproj_sample(
            xs[0], mod_s[l], g_mix, w_in_bf, conv_w[l], time_major(state_conv[l]), n_q)
        k_sm, v_sm = seq_major(k_s, n_q), seq_major(v_s, n_q)
        pad = ((0, 0), (0, 8 - n_q), (0, 0))
        att_s = _attn_sample(page_table, seq_major(q_s, n_q), jnp.pad(k_sm, pad), jnp.pad(v_sm, pad),
                             bias_s, cache_kt, cache_vt, l)
        x1s, h2s = _post(xs, time_major(att_s)[None], cb_s[None], ga_s[None], gb_s[None], mod_s[l], g_ffn,
                         wa, wc, wo, tr=rows_s, att_transposed=False)
        xs, fstate_s = _ffn(x1s, h2s, mod_s[l], wu, ffn_conv_w[l], cbias, wd, time_major(state_ffn[l]),
                            final_g, tr=rows_s, stride=n_seq)
        outs[4].append(k_sm.reshape(n_seq, n_q, N_HEADS, HEAD_DIM))
        outs[5].append(v_sm.reshape(n_seq, n_q, N_HEADS, HEAD_DIM))
        outs[6].append(seq_major(cstate_s, 2))
        outs[7].append(seq_major(fstate_s, 2))

    y_sample = seq_major(xs[0], n_q)
    return (xp, y_sample) + tuple(jnp.stack(o) for o in outs)
```

```python
import functools

import jax
import jax.numpy as jnp
from jax import lax
from jax.experimental import pallas as pl
from jax.experimental.pallas import tpu as pltpu

D_MODEL = 1024
N_HEADS = 8
HEAD_DIM = 64
ATT_WIDTH = N_HEADS * HEAD_DIM
CONV_WIDTH = 512
BLOCK = 256
TOP_K = 3
N_BUCKETS = 32
MAX_DIST = 1024
D_FF = 2816
PAGE_SIZE = 128
EPS = 1e-6
NEG = -1e30
REMOVED = -3e38
LOG2E = 1.4426950408889634
N_IN = 3 * ATT_WIDTH + 3 * CONV_WIDTH + 2 * D_MODEL
LANES = 128
FFN_CHUNK = 256
N_BIAS_TILES = 6
MAX_BLOCKS = 32
V_ROWS = HEAD_DIM + 16
VMEM_LIMIT = 56 * 1024 * 1024

_BF = jnp.bfloat16
_F32 = jnp.float32


def _bucket_upper_bounds():
    max_exact = N_BUCKETS // 2
    n_log = N_BUCKETS - max_exact
    ratio = MAX_DIST // max_exact
    assert ratio * max_exact == MAX_DIST
    ups = [b + 1 for b in range(max_exact)]
    for k in range(1, n_log):
        target = (max_exact ** n_log) * (ratio ** k)
        n = max_exact
        while n ** n_log < target:
            n += 1
        ups.append(n)
    return ups


_BUCKET_UPPER = _bucket_upper_bounds()
assert (N_BIAS_TILES - 2) * BLOCK + 1 >= _BUCKET_UPPER[-1]


def _dot(a, b):
    return jnp.dot(a, b, preferred_element_type=_F32)


def _dot_nt(a, b):
    return lax.dot_general(a, b, (((1,), (1,)), ((), ())), preferred_element_type=_F32)


def _sigmoid(x):
    return 1.0 / (1.0 + jnp.exp(-x))


def _silu(x):
    return x * _sigmoid(x)


def _rmsnorm(x, g):
    return x * lax.rsqrt(jnp.mean(x * x, axis=-1, keepdims=True) + EPS) * g


def _mod_chunk(mod_ref, idx, reps):
    v = mod_ref[0, :, idx * D_MODEL:(idx + 1) * D_MODEL]
    if reps > 1:
        v = jnp.concatenate([v] * reps, axis=0)
    return v


def _params(sem, vmem=VMEM_LIMIT):
    return pltpu.CompilerParams(dimension_semantics=sem, vmem_limit_bytes=vmem)


def _const_spec(shape):
    nd = len(shape)
    return pl.BlockSpec(shape, lambda *_: (0,) * nd, pipeline_mode=pl.Buffered(1))


def _layer_spec(stacked, layer):
    return pl.BlockSpec((1,) + stacked.shape[1:], lambda *_: (layer, 0, 0), pipeline_mode=pl.Buffered(1))


def _mod_kernel(c_ref, w_ref, b_ref, o_ref):
    a = _silu(c_ref[...]).astype(_BF)
    o_ref[0] = _dot(a, w_ref[0].astype(_BF)) + b_ref[0]


def _modulation(c_all, ada_w, ada_b):
    depth = ada_w.shape[0]
    rows = c_all.shape[0]
    n_chunks = ada_w.shape[2] // D_MODEL
    return pl.pallas_call(
        _mod_kernel,
        grid=(depth, n_chunks),
        in_specs=[
            pl.BlockSpec((rows, D_MODEL), lambda l, n: (0, 0)),
            pl.BlockSpec((1, D_MODEL, D_MODEL), lambda l, n: (l, 0, n)),
            pl.BlockSpec((1, 1, D_MODEL), lambda l, n: (l, 0, n)),
        ],
        out_specs=pl.BlockSpec((1, rows, D_MODEL), lambda l, n: (l, 0, n)),
        out_shape=jax.ShapeDtypeStruct((depth, rows, ada_w.shape[2]), _F32),
        compiler_params=_params(("arbitrary", "arbitrary")),
        name="modulation",
    )(c_all, ada_w, ada_b.reshape(depth, 1, -1))


def _bias_kernel(rb_ref, bp_ref, bs_ref, *, past_len):
    h = pl.program_id(0)

    def table(dist, scale):
        n = jnp.maximum(dist, 0)
        v = jnp.full(dist.shape, rb_ref[N_BUCKETS - 1, h], _F32)
        for b in range(N_BUCKETS - 2, -1, -1):
            v = jnp.where(n < _BUCKET_UPPER[b], rb_ref[b, h], v)
        return jnp.where(dist >= 0, v * scale, NEG)

    kk = lax.broadcasted_iota(jnp.int32, (BLOCK, BLOCK), 0)
    qq = lax.broadcasted_iota(jnp.int32, (BLOCK, BLOCK), 1)
    for d in range(N_BIAS_TILES):
        bp_ref[0, d] = table(d * BLOCK + qq - kk, LOG2E)
    t = lax.broadcasted_iota(jnp.int32, bs_ref.shape[1:], 0)
    p = lax.broadcasted_iota(jnp.int32, bs_ref.shape[1:], 1)
    bs_ref[0] = table(past_len + t - p, 1.0)


def _bias_tables(rel_bias, past_len):
    s_cols = past_len + PAGE_SIZE
    return pl.pallas_call(
        functools.partial(_bias_kernel, past_len=past_len),
        grid=(N_HEADS,),
        in_specs=[pl.BlockSpec(memory_space=pltpu.SMEM)],
        out_specs=[
            pl.BlockSpec((1, N_BIAS_TILES, BLOCK, BLOCK), lambda h: (h, 0, 0, 0)),
            pl.BlockSpec((1, 8, s_cols), lambda h: (h, 0, 0)),
        ],
        out_shape=[
            jax.ShapeDtypeStruct((N_HEADS, N_BIAS_TILES, BLOCK, BLOCK), _F32),
            jax.ShapeDtypeStruct((N_HEADS, 8, s_cols), _F32),
        ],
        compiler_params=_params(("arbitrary",)),
        name="bias_tables",
    )(rel_bias)


def _modulated_norm(x, g, sh, sc):
    return (_rmsnorm(x, g) * (1.0 + sc) + sh).astype(_BF)


def _project(hb, w_ref):
    bounds = (0, ATT_WIDTH, 2 * ATT_WIDTH, 3 * ATT_WIDTH, 3 * ATT_WIDTH + CONV_WIDTH,
              3 * ATT_WIDTH + 2 * CONV_WIDTH, 3 * ATT_WIDTH + 3 * CONV_WIDTH,
              3 * ATT_WIDTH + 3 * CONV_WIDTH + D_MODEL, N_IN)
    return [_dot(hb, w_ref[0, :, a:b]) for a, b in zip(bounds[:-1], bounds[1:])]


def _inproj_prompt_kernel(x_ref, mod_ref, g_ref, w_ref, wkx_ref, cw_ref,
                          k_ref, v_ref, kx_ref, qtz_ref, qt_ref, vt_ref, kmean_ref, cb_ref, ga_ref, gb_ref,
                          cstate_ref, ext_ref):
    tr = x_ref.shape[1]
    t = pl.program_id(1)

    @pl.when(t == 0)
    def _():
        ext_ref[0:8, :] = jnp.zeros((8, CONV_WIDTH), _F32)

    hb = _modulated_norm(x_ref[0], g_ref[...], _mod_chunk(mod_ref, 0, 1), _mod_chunk(mod_ref, 1, 1))
    q, k, v, u, bg, cg, ga, gb = _project(hb, w_ref)

    qt = jnp.transpose(q * (HEAD_DIM ** -0.5))
    zeros = jnp.zeros((HEAD_DIM, tr), _F32)
    for h in range(N_HEADS):
        qh = qt[h * HEAD_DIM:(h + 1) * HEAD_DIM]
        pair = [qh, zeros] if h % 2 == 0 else [zeros, qh]
        qtz_ref[0, h] = jnp.concatenate(pair, axis=0).astype(_BF)
    qt_ref[0] = (qt * LOG2E).astype(_BF)

    k_ref[0] = k
    kmean_ref[0, 0] = jnp.mean(k, axis=0, keepdims=True)
    lane = lax.broadcasted_iota(jnp.int32, (tr, N_HEADS * LANES), 1) % LANES
    block_lanes = (lane == HEAD_DIM + t) | (lane == HEAD_DIM + MAX_BLOCKS + t)
    kx = jnp.where(block_lanes, 1.0, _dot(hb, wkx_ref[0])).astype(_BF)
    for h in range(N_HEADS):
        kx_ref[0, 0, h] = kx[:, h * LANES:(h + 1) * LANES]

    v_ref[0] = v
    vt = jnp.transpose(v)
    ones = jnp.ones((V_ROWS - HEAD_DIM, tr), _BF)
    for h in range(N_HEADS):
        vt_ref[0, h, 0] = jnp.concatenate([vt[h * HEAD_DIM:(h + 1) * HEAD_DIM].astype(_BF), ones], axis=0)

    ext_ref[8:8 + tr, :] = cg * u
    cw = cw_ref[...]
    conv = (cw[0:1] * ext_ref[6:6 + tr, :] + cw[1:2] * ext_ref[7:7 + tr, :]
            + cw[2:3] * ext_ref[8:8 + tr, :])
    cb_ref[0] = (bg * conv).astype(_BF)
    cstate_ref[0] = ext_ref[tr + 6:tr + 8, :]
    ext_ref[0:8, :] = ext_ref[tr:tr + 8, :]

    ga_ref[0] = _sigmoid(ga)
    gb_ref[0] = _sigmoid(gb)


def _inproj_prompt(x, mod, g, w_bf, wkx_bf, cw, layer):
    bsz, seq, _ = x.shape
    tr = BLOCK
    nt = seq // tr
    assert nt == MAX_BLOCKS
    row = lambda w: pl.BlockSpec((1, tr, w), lambda b, t: (b, t, 0))
    return pl.pallas_call(
        _inproj_prompt_kernel,
        grid=(bsz, nt),
        in_specs=[
            row(D_MODEL),
            pl.BlockSpec((1, 1, mod.shape[2]), lambda b, t: (b, 0, 0)),
            _const_spec((1, D_MODEL)),
            _layer_spec(w_bf, layer),
            _layer_spec(wkx_bf, layer),
            _const_spec((3, CONV_WIDTH)),
        ],
        out_specs=[
            row(ATT_WIDTH), row(ATT_WIDTH),
            pl.BlockSpec((1, 1, N_HEADS, tr, LANES), lambda b, t: (b, t, 0, 0, 0)),
            pl.BlockSpec((1, N_HEADS, 2 * HEAD_DIM, tr), lambda b, t: (b, 0, 0, t)),
            pl.BlockSpec((1, ATT_WIDTH, tr), lambda b, t: (b, 0, t)),
            pl.BlockSpec((1, N_HEADS, 1, V_ROWS, tr), lambda b, t: (b, 0, t, 0, 0)),
            pl.BlockSpec((1, 1, 1, ATT_WIDTH), lambda b, t: (b, t, 0, 0)),
            row(CONV_WIDTH), row(D_MODEL), row(D_MODEL),
            pl.BlockSpec((1, 2, CONV_WIDTH), lambda b, t: (b, 0, 0)),
        ],
        out_shape=[
            jax.ShapeDtypeStruct((bsz, seq, ATT_WIDTH), _F32),
            jax.ShapeDtypeStruct((bsz, seq, ATT_WIDTH), _F32),
            jax.ShapeDtypeStruct((bsz, nt, N_HEADS, tr, LANES), _BF),
            jax.ShapeDtypeStruct((bsz, N_HEADS, 2 * HEAD_DIM, seq), _BF),
            jax.ShapeDtypeStruct((bsz, ATT_WIDTH, seq), _BF),
            jax.ShapeDtypeStruct((bsz, N_HEADS, nt, V_ROWS, tr), _BF),
            jax.ShapeDtypeStruct((bsz, nt, 1, ATT_WIDTH), _F32),
            jax.ShapeDtypeStruct((bsz, seq, CONV_WIDTH), _BF),
            jax.ShapeDtypeStruct((bsz, seq, D_MODEL), _F32),
            jax.ShapeDtypeStruct((bsz, seq, D_MODEL), _F32),
            jax.ShapeDtypeStruct((bsz, 2, CONV_WIDTH), _F32),
        ],
        scratch_shapes=[pltpu.VMEM((tr + 8, CONV_WIDTH), _F32)],
        compiler_params=_params(("arbitrary", "arbitrary")),
        name="inproj_prompt",
    )(x, mod, g, w_bf, wkx_bf, cw)


def _inproj_sample_kernel(x_ref, mod_ref, g_ref, w_ref, cw_ref, prev_ref,
                          q_ref, k_ref, v_ref, cb_ref, ga_ref, gb_ref, cstate_ref, *, reps):
    rows = x_ref.shape[0]
    stride = rows // reps
    hb = _modulated_norm(x_ref[...], g_ref[...], _mod_chunk(mod_ref, 0, reps), _mod_chunk(mod_ref, 1, reps))
    q, k, v, u, bg, cg, ga, gb = _project(hb, w_ref)
    q_ref[...] = q * (HEAD_DIM ** -0.5)
    k_ref[...] = k
    v_ref[...] = v
    cu = cg * u
    ext = jnp.concatenate([prev_ref[...], cu], axis=0)
    cw = cw_ref[...]
    conv = (cw[0:1] * ext[0:rows] + cw[1:2] * ext[stride:stride + rows]
            + cw[2:3] * ext[2 * stride:2 * stride + rows])
    cb_ref[...] = (bg * conv).astype(_BF)
    cstate_ref[...] = ext[rows:rows + 2 * stride]
    ga_ref[...] = _sigmoid(ga)
    gb_ref[...] = _sigmoid(gb)


def _inproj_sample(x, mod, g, w_bf, cw, prev, reps, layer):
    rows = x.shape[0]
    full = lambda a: pl.BlockSpec(a.shape, lambda i: (0,) * a.ndim)
    shp = lambda w, dt=_F32: jax.ShapeDtypeStruct((rows, w), dt)
    outs = [shp(ATT_WIDTH), shp(ATT_WIDTH), shp(ATT_WIDTH), shp(CONV_WIDTH, _BF),
            shp(D_MODEL), shp(D_MODEL), jax.ShapeDtypeStruct(prev.shape, _F32)]
    return pl.pallas_call(
        functools.partial(_inproj_sample_kernel, reps=reps),
        grid=(1,),
        in_specs=[full(x), full(mod), full(g), _layer_spec(w_bf, layer), full(cw), full(prev)],
        out_specs=[pl.BlockSpec(o.shape, lambda i: (0, 0)) for o in outs],
        out_shape=outs,
        compiler_params=_params(("arbitrary",)),
        name="inproj_sample",
    )(x, mod, g, w_bf, cw, prev)


def _top_k_rows(gate, own):
    n = gate.shape[0]
    row = lax.broadcasted_iota(jnp.int32, gate.shape, 0).astype(_F32)
    past = row < own
    gate = jnp.where(past, gate, NEG)
    mask = jnp.where(row == own, 0.0, NEG)
    for _ in range(min(TOP_K, n)):
        mx = jnp.max(gate, axis=0, keepdims=True)
        first = jnp.min(jnp.where(gate == mx, row, float(n)), axis=0, keepdims=True)
        chosen = row == first
        mask = jnp.where(chosen, jnp.where(past, 0.0, mask), mask)
        gate = jnp.where(chosen, REMOVED, gate)
    return mask


def _attn_prompt_kernel(qtz_ref, qt_ref, kx_ref, vt_ref, kmean_ref, bias_ref, o_ref,
                        qx_ref, m_ref, acc_ref, s_ref):
    i = pl.program_id(1)
    tq = qt_ref.shape[2]
    own = i.astype(_F32)
    n_far = jnp.maximum(i - (N_BIAS_TILES - 2), 0)
    is_far = lax.broadcasted_iota(jnp.int32, (MAX_BLOCKS, tq), 0) < n_far
    for h in range(N_HEADS):
        km = kmean_ref[0, :, (h // 2) * LANES:(h // 2 + 1) * LANES].astype(_BF)
        mask = _top_k_rows(_dot(km, qtz_ref[0, h]), own)
        far_bias = bias_ref[h, N_BIAS_TILES - 1, 0:1, 0:1]
        ext = mask + jnp.where(is_far, far_bias, 0.0)
        hi = ext.astype(_BF)
        qx_ref[h, 0:HEAD_DIM] = qt_ref[0, h * HEAD_DIM:(h + 1) * HEAD_DIM, :]
        qx_ref[h, HEAD_DIM:HEAD_DIM + MAX_BLOCKS] = hi
        qx_ref[h, HEAD_DIM + MAX_BLOCKS:] = (ext - hi.astype(_F32)).astype(_BF)
    m_ref[...] = jnp.full(m_ref.shape, NEG, _F32)
    acc_ref[...] = jnp.zeros(acc_ref.shape, _F32)

    n_pairs = N_HEADS // 2
    last_block = kx_ref.shape[1] - 1

    def logits(j, pair):
        return jnp.concatenate(
            [_dot(kx_ref[0, j, 2 * pair + e], qx_ref[2 * pair + e]) for e in range(2)], axis=1)

    ahead = s_ref.shape[0]
    for pair in range(ahead):
        s_ref[pair] = logits(0, pair)

    def body(j, carry, near):
        pending = [s_ref[pair] for pair in range(ahead)]
        for pair in range(n_pairs):
            s2 = pending.pop(0)
            if pair + ahead < n_pairs:
                pending.append(logits(j, pair + ahead))
            else:
                s_ref[pair + ahead - n_pairs] = logits(jnp.minimum(j + 1, last_block), pair + ahead - n_pairs)
            for e in range(2):
                h = 2 * pair + e
                s = s2[:, e * tq:(e + 1) * tq]
                if near:
                    s = s + bias_ref[h, i - j]
                m_old = m_ref[h:h + 1, :]
                m_new = jnp.maximum(m_old, jnp.max(s, axis=0, keepdims=True))
                alpha = jnp.exp2(m_old - m_new)
                p = jnp.exp2(s - m_new)
                m_ref[h:h + 1, :] = m_new
                acc_ref[h] = alpha * acc_ref[h] + _dot(vt_ref[0, h, j], p.astype(_BF))
        return carry

    def run(lo, hi, near):
        def two_blocks(jj, carry):
            body(lo + 2 * jj, carry, near)
            return body(lo + 2 * jj + 1, carry, near)

        lax.fori_loop(0, (hi - lo) // 2, two_blocks, 0)

        @pl.when((hi - lo) % 2 == 1)
        def _():
            body(hi - 1, 0, near)

    run(0, n_far, False)
    run(n_far, i + 1, True)
    for h in range(N_HEADS):
        acc = acc_ref[h]
        o_ref[0, h * HEAD_DIM:(h + 1) * HEAD_DIM, :] = (acc[:HEAD_DIM] / acc[HEAD_DIM:HEAD_DIM + 1]).astype(_BF)


def _attn_prompt(qtz, qt, kx, vt, kmean, bias_p):
    bsz, _, seq = qt.shape
    nb = seq // BLOCK
    km3 = kmean.reshape(bsz, nb, ATT_WIDTH)
    per_batch = lambda shape: pl.BlockSpec(
        (1,) + shape[1:], lambda b, i: (b,) + (0,) * (len(shape) - 1), pipeline_mode=pl.Buffered(1))
    return pl.pallas_call(
        _attn_prompt_kernel,
        grid=(bsz, nb),
        in_specs=[
            pl.BlockSpec((1, N_HEADS, 2 * HEAD_DIM, BLOCK), lambda b, i: (b, 0, 0, i)),
            pl.BlockSpec((1, ATT_WIDTH, BLOCK), lambda b, i: (b, 0, i)),
            per_batch(kx.shape), per_batch(vt.shape), per_batch(km3.shape),
            _const_spec(bias_p.shape),
        ],
        out_specs=pl.BlockSpec((1, ATT_WIDTH, BLOCK), lambda b, i: (b, 0, i)),
        out_shape=jax.ShapeDtypeStruct((bsz, ATT_WIDTH, seq), _BF),
        scratch_shapes=[
            pltpu.VMEM((N_HEADS, LANES, BLOCK), _BF),
            pltpu.VMEM((N_HEADS, BLOCK), _F32),
            pltpu.VMEM((N_HEADS, V_ROWS, BLOCK), _F32),
            pltpu.VMEM((2, BLOCK, 2 * BLOCK), _F32),
        ],
        compiler_params=_params(("arbitrary", "arbitrary")),
        name="attn_prompt",
    )(qtz, qt, kx, vt, km3, bias_p)


def _attn_sample_kernel(pt_ref, q_ref, kn_ref, vn_ref, bias_ref, *refs, n_pages, n_q, group):
    del pt_ref
    o_ref = refs[2 * group * n_pages]
    for s in range(group):
        k_refs = refs[s * n_pages:(s + 1) * n_pages]
        v_refs = refs[(group + s) * n_pages:(group + s + 1) * n_pages]
        o_ref[:, s, 0, :] = _attend_sample_seq(
            q_ref[:, s, 0, :], kn_ref[:, s, 0, :], vn_ref[:, s, 0, :], bias_ref, k_refs, v_refs, n_q)


def _rows_to_page(x, n_rows):
    row = lax.broadcasted_iota(jnp.int32, (PAGE_SIZE, x.shape[1]), 0)
    out = jnp.zeros((PAGE_SIZE, x.shape[1]), _F32)
    for t in range(n_rows):
        out = jnp.where(row == t, x[t:t + 1], out)
    return out.astype(_BF)


def _attend_sample_seq(q, k_new, v_new, bias_ref, k_refs, v_refs, n_q):
    n_pages = len(k_refs)
    rows = n_q * N_HEADS
    pages_per_block = BLOCK // PAGE_SIZE
    n_blocks = n_pages // pages_per_block

    rep = jnp.concatenate([jnp.broadcast_to(q[t:t + 1], (N_HEADS, ATT_WIDTH)) for t in range(n_q)], axis=0)
    r_head = lax.broadcasted_iota(jnp.int32, (rows, ATT_WIDTH), 0) % N_HEADS
    c_head = lax.broadcasted_iota(jnp.int32, (rows, ATT_WIDTH), 1) // HEAD_DIM
    head_lanes = r_head == c_head
    q_rows = jnp.where(head_lanes, rep, 0.0).astype(_BF)

    kt_pages = [r[0, 0] for r in k_refs]
    block_sums = [
        jnp.sum(sum(kt_pages[n * pages_per_block:(n + 1) * pages_per_block]), axis=1, keepdims=True)
        for n in range(n_blocks)]
    kmean_t = jnp.concatenate(block_sums, axis=1) * (1.0 / BLOCK)
    gate = _dot(q_rows, kmean_t.astype(_BF))

    col = lax.broadcasted_iota(jnp.int32, gate.shape, 1).astype(_F32)
    sel = jnp.full(gate.shape, NEG, _F32)
    for _ in range(min(TOP_K, n_blocks)):
        mx = jnp.max(gate, axis=1, keepdims=True)
        first = jnp.min(jnp.where(gate == mx, col, float(n_blocks)), axis=1, keepdims=True)
        chosen = col == first
        sel = jnp.where(chosen, 0.0, sel)
        gate = jnp.where(chosen, REMOVED, gate)

    logits = []
    for p in range(n_pages):
        s = _dot(q_rows, kt_pages[p].astype(_BF)) + bias_ref[:, p * PAGE_SIZE:(p + 1) * PAGE_SIZE]
        n = p // pages_per_block
        logits.append(s + sel[:, n:n + 1])
    k_own = _rows_to_page(k_new, n_q)
    v_own = _rows_to_page(v_new, n_q)
    logits.append(_dot_nt(q_rows, k_own) + bias_ref[:, n_pages * PAGE_SIZE:])

    m = functools.reduce(jnp.maximum, logits)
    m = jnp.max(m, axis=1, keepdims=True)
    l = jnp.zeros((rows, 1), _F32)
    acc = jnp.zeros((rows, ATT_WIDTH), _F32)
    for p in range(n_pages + 1):
        pr = jnp.exp(logits[p] - m)
        l = l + jnp.sum(pr, axis=1, keepdims=True)
        if p == n_pages:
            acc = acc + _dot(pr.astype(_BF), v_own)
        else:
            acc = acc + _dot_nt(pr.astype(_BF), v_refs[p][0, 0].astype(_BF))
    out = jnp.where(head_lanes, acc / l, 0.0)
    return jnp.sum(out.reshape(n_q, N_HEADS, ATT_WIDTH), axis=1)


def _attn_sample(page_table, q, k_new, v_new, bias_s, cache_kt, cache_vt, layer):
    n_seq, n_pages = page_table.shape
    n_q = q.shape[0]
    group = 2
    assert n_seq % group == 0
    page_spec = lambda s, p: pl.BlockSpec(
        (1, 1, ATT_WIDTH, PAGE_SIZE), lambda g, pt: (layer, pt[group * g + s, p], 0, 0))
    pages = [page_spec(s, p) for s in range(group) for p in range(n_pages)]
    per_group = pl.BlockSpec((n_q, group, 1, ATT_WIDTH), lambda g, pt: (0, g, 0, 0))
    grid_spec = pltpu.PrefetchScalarGridSpec(
        num_scalar_prefetch=1,
        grid=(n_seq // group,),
        in_specs=[per_group, per_group, per_group,
                  pl.BlockSpec(bias_s.shape, lambda g, pt: (0, 0), pipeline_mode=pl.Buffered(1))]
                 + pages * 2,
        out_specs=per_group,
    )
    n_in = group * n_pages
    return pl.pallas_call(
        functools.partial(_attn_sample_kernel, n_pages=n_pages, n_q=n_q, group=group),
        grid_spec=grid_spec,
        out_shape=jax.ShapeDtypeStruct(q.shape, _F32),
        compiler_params=_params(("arbitrary",)),
        name="attn_sample",
    )(page_table, q, k_new, v_new, bias_s, *([cache_kt] * n_in), *([cache_vt] * n_in))


def _post_kernel(x_ref, att_ref, cb_ref, ga_ref, gb_ref, mod_ref, g_ref, wa_ref, wc_ref, wo_ref,
                 x1_ref, h2_ref, *, reps, att_transposed):
    if att_transposed:
        att = jnp.transpose(att_ref[0].astype(_F32)).astype(_BF)
    else:
        att = att_ref[0].astype(_BF)
    y_a = _dot(att, wa_ref[0])
    y_b = _dot(cb_ref[0], wc_ref[0])
    merged = ga_ref[0] * y_a + gb_ref[0] * y_b
    x1 = x_ref[0] + _mod_chunk(mod_ref, 2, reps) * _dot(merged.astype(_BF), wo_ref[0])
    x1_ref[0] = x1
    h2 = _rmsnorm(x1, g_ref[...]) * (1.0 + _mod_chunk(mod_ref, 4, reps)) + _mod_chunk(mod_ref, 3, reps)
    h2_ref[0] = h2.astype(_BF)


def _post(x, att, cb, ga, gb, mod, g, wa, wc, wo, *, layer, tr, att_transposed):
    bsz, seq, _ = x.shape
    nt = seq // tr
    reps = 1 if mod.shape[1] == 1 else tr // mod.shape[1]
    row = lambda w: pl.BlockSpec((1, tr, w), lambda b, t: (b, t, 0))
    att_spec = (pl.BlockSpec((1, ATT_WIDTH, tr), lambda b, t: (b, 0, t)) if att_transposed
                else row(ATT_WIDTH))
    return pl.pallas_call(
        functools.partial(_post_kernel, reps=reps, att_transposed=att_transposed),
        grid=(bsz, nt),
        in_specs=[
            row(D_MODEL), att_spec, row(CONV_WIDTH), row(D_MODEL), row(D_MODEL),
            pl.BlockSpec((1,) + mod.shape[1:], lambda b, t: (b, 0, 0)),
            _const_spec((1, D_MODEL)),
            _layer_spec(wa, layer), _layer_spec(wc, layer), _layer_spec(wo, layer),
        ],
        out_specs=[row(D_MODEL), row(D_MODEL)],
        out_shape=[jax.ShapeDtypeStruct(x.shape, _F32), jax.ShapeDtypeStruct(x.shape, _BF)],
        compiler_params=_params(("arbitrary", "arbitrary")),
        name="post_attn",
    )(x, att, cb, ga, gb, mod, g, wa, wc, wo)


def _ffn_kernel(*refs, reps, stride, final):
    if stride == 1:
        x1_ref, h2_ref, mod_ref, wu_ref, cw_ref, cbias_ref, wd_ref = refs[:7]
        rest = refs[7:]
        prev_ref = None
    else:
        x1_ref, h2_ref, mod_ref, wu_ref, cw_ref, cbias_ref, wd_ref, prev_ref = refs[:8]
        rest = refs[8:]
    if final:
        fg_ref, rest = rest[0], rest[1:]
    out_ref, state_ref = rest[0], rest[1]
    ext_ref = rest[2] if stride == 1 else None

    rows = x1_ref.shape[1]
    hb = h2_ref[0]
    cw = cw_ref[...]
    cbias = cbias_ref[...]

    if stride == 1:
        @pl.when(pl.program_id(1) == 0)
        def _():
            ext_ref[0:8, :] = jnp.zeros((8, 2 * D_FF), _F32)

    n_chunks = D_FF // FFN_CHUNK
    a_cols = lambda c: slice(c * FFN_CHUNK, (c + 1) * FFN_CHUNK)
    b_cols = lambda c: slice(D_FF + c * FFN_CHUNK, D_FF + (c + 1) * FFN_CHUNK)
    up_pair = lambda c: (_dot(hb, wu_ref[0, :, a_cols(c)]), _dot(hb, wu_ref[0, :, b_cols(c)]))

    def conv_cols(up, sl):
        if stride == 1:
            ext_ref[8:8 + rows, sl] = up
            e0, e1, e2 = ext_ref[6:6 + rows, sl], ext_ref[7:7 + rows, sl], up
        else:
            ext = jnp.concatenate([prev_ref[:, sl], up], axis=0)
            e0, e1, e2 = ext[0:rows], ext[stride:stride + rows], up
            state_ref[:, sl] = ext[rows:rows + 2 * stride]
        return cw[0:1, sl] * e0 + cw[1:2, sl] * e1 + cw[2:3, sl] * e2 + cbias[:, sl]

    acc = jnp.zeros((rows, D_MODEL), _F32)
    up_next = up_pair(0)
    for c in range(n_chunks):
        up_a, up_b = up_next
        if c + 1 < n_chunks:
            up_next = up_pair(c + 1)
        a = conv_cols(up_a, a_cols(c))
        b = conv_cols(up_b, b_cols(c))
        acc = acc + _dot((_silu(a) * b).astype(_BF), wd_ref[0, a_cols(c), :])

    if stride == 1:
        state_ref[0] = ext_ref[rows + 6:rows + 8, :]
        ext_ref[0:8, :] = ext_ref[rows:rows + 8, :]

    x2 = x1_ref[0] + _mod_chunk(mod_ref, 5, reps) * acc
    out_ref[0] = _rmsnorm(x2, fg_ref[...]) if final else x2


def _ffn(x1, h2, mod, wu, cw, cbias, wd, prev, final_g, *, layer, tr, stride):
    bsz, seq, _ = x1.shape
    nt = seq // tr
    reps = 1 if mod.shape[1] == 1 else tr // mod.shape[1]
    final = final_g is not None
    row = lambda w: pl.BlockSpec((1, tr, w), lambda b, t: (b, t, 0))
    in_specs = [row(D_MODEL), row(D_MODEL),
                pl.BlockSpec((1,) + mod.shape[1:], lambda b, t: (b, 0, 0)),
                _layer_spec(wu, layer), _const_spec(cw.shape), _const_spec(cbias.shape), _layer_spec(wd, layer)]
    args = [x1, h2, mod, wu, cw, cbias, wd]
    if stride == 1:
        state_shape = jax.ShapeDtypeStruct((bsz, 2, 2 * D_FF), _F32)
        state_spec = pl.BlockSpec((1, 2, 2 * D_FF), lambda b, t: (b, 0, 0))
        scratch = [pltpu.VMEM((tr + 8, 2 * D_FF), _F32)]
    else:
        assert bsz == 1 and nt == 1 and tr % stride == 0
        in_specs.append(_const_spec(prev.shape))
        args.append(prev)
        state_shape = jax.ShapeDtypeStruct(prev.shape, _F32)
        state_spec = pl.BlockSpec(prev.shape, lambda b, t: (0, 0))
        scratch = []
    if final:
        in_specs.append(_const_spec(final_g.shape))
        args.append(final_g)
    return pl.pallas_call(
        functools.partial(_ffn_kernel, reps=reps, stride=stride, final=final),
        grid=(bsz, nt),
        in_specs=in_specs,
        out_specs=[row(D_MODEL), state_spec],
        out_shape=[jax.ShapeDtypeStruct(x1.shape, _F32), state_shape],
        scratch_shapes=scratch,
        compiler_params=_params(("arbitrary", "arbitrary")),
        name="conv_ffn",
    )(*args)


def kernel(x_prompt, x_sample, c_prompt, c_sample, cache_k, cache_v, state_conv, state_ffn, page_table,
           rel_bias, ada_w, ada_b, norm_mix_g, norm_ffn_g, final_norm_g, w_in, conv_w, w_att_out,
           w_conv_out, w_o, w_up, ffn_conv_w, ffn_conv_b, w_down):
    depth = w_in.shape[0]
    bsz, seq, _ = x_prompt.shape
    n_seq, n_q, _ = x_sample.shape
    n_pages = page_table.shape[1]
    past_len = n_pages * PAGE_SIZE
    assert seq % BLOCK == 0 and past_len % BLOCK == 0 and n_q <= 8 and bsz <= 8
    assert cache_k.shape[2] == PAGE_SIZE

    pad_rows = 8
    c_all = jnp.concatenate([c_prompt, jnp.zeros((pad_rows - bsz, D_MODEL), _F32), c_sample], axis=0)
    mod_all = _modulation(c_all, ada_w, ada_b)
    mod_p = mod_all[:, :bsz].reshape(depth, bsz, 1, -1)
    mod_s = mod_all[:, pad_rows:].reshape(depth, 1, n_seq, -1)

    bias_p, bias_s8 = _bias_tables(rel_bias, past_len)
    bias_s = bias_s8[:, :n_q].transpose(1, 0, 2).reshape(n_q * N_HEADS, -1)

    page_t = lambda c: c.transpose(0, 1, 3, 4, 2).reshape(c.shape[0], c.shape[1], ATT_WIDTH, PAGE_SIZE)
    cache_kt, cache_vt = page_t(cache_k), page_t(cache_v)

    time_major = lambda a: a.transpose(1, 0, 2).reshape(-1, a.shape[2])
    seq_major = lambda a, steps: a.reshape(steps, n_seq, -1).transpose(1, 0, 2)
    rows_s = n_q * n_seq

    w_in_bf = w_in.astype(_BF)
    wk = w_in[:, :, ATT_WIDTH:2 * ATT_WIDTH].reshape(depth, D_MODEL, N_HEADS, HEAD_DIM)
    wkx_bf = jnp.pad(wk, ((0, 0), (0, 0), (0, 0), (0, LANES - HEAD_DIM))).reshape(depth, D_MODEL, -1).astype(_BF)
    wa, wc, wo = w_att_out.astype(_BF), w_conv_out.astype(_BF), w_o.astype(_BF)
    wu, wd = w_up.astype(_BF), w_down.astype(_BF)
    conv_prev = state_conv.transpose(0, 2, 1, 3).reshape(depth, 2 * n_seq, -1)
    ffn_prev = state_ffn.transpose(0, 2, 1, 3).reshape(depth, 2 * n_seq, -1)
    by_seq = lambda a: a.reshape(n_q, n_seq, 1, -1)

    xp = x_prompt
    xs = time_major(x_sample)[None]
    fin = final_norm_g.reshape(1, D_MODEL)
    outs = [[] for _ in range(8)]
    for l in range(depth):
        g_mix = norm_mix_g[l].reshape(1, D_MODEL)
        g_ffn = norm_ffn_g[l].reshape(1, D_MODEL)
        cbias = ffn_conv_b[l].reshape(1, -1)
        final_g = fin if l == depth - 1 else None

        k, v, kx, qtz, qt, vt, kmean, cb, ga, gb, cstate = _inproj_prompt(
            xp, mod_p[l], g_mix, w_in_bf, wkx_bf, conv_w[l], l)
        att_t = _attn_prompt(qtz, qt, kx, vt, kmean, bias_p)
        x1, h2 = _post(xp, att_t, cb, ga, gb, mod_p[l], g_ffn, wa, wc, wo,
                       layer=l, tr=BLOCK, att_transposed=True)
        xp, fstate = _ffn(x1, h2, mod_p[l], wu, ffn_conv_w[l], cbias, wd, None, final_g,
                          layer=l, tr=BLOCK, stride=1)
        for dst, val in zip(outs[:4], (k, v, cstate, fstate)):
            dst.append(val)

        q_s, k_s, v_s, cb_s, ga_s, gb_s, cstate_s = _inproj_sample(
            xs[0], mod_s[l], g_mix, w_in_bf, conv_w[l], conv_prev[l], n_q, l)
        att_s = _attn_sample(page_table, by_seq(q_s), by_seq(k_s), by_seq(v_s), bias_s, cache_kt, cache_vt, l)
        x1s, h2s = _post(xs, att_s.reshape(1, rows_s, ATT_WIDTH), cb_s[None], ga_s[None], gb_s[None],
                         mod_s[l], g_ffn, wa, wc, wo, layer=l, tr=rows_s, att_transposed=False)
        xs, fstate_s = _ffn(x1s, h2s, mod_s[l], wu, ffn_conv_w[l], cbias, wd, ffn_prev[l], final_g,
                            layer=l, tr=rows_s, stride=n_seq)
        for dst, val in zip(outs[4:], (k_s, v_s, cstate_s, fstate_s)):
            dst.append(val)

    kp, vp, cp, fp, ks, vs, cs, fs = (jnp.stack(o) for o in outs)
    heads = lambda a: a.reshape(a.shape[:-1] + (N_HEADS, HEAD_DIM))
    to_seq_major = lambda a, steps: a.reshape(depth, steps, n_seq, -1).transpose(0, 2, 1, 3)
    return (xp, seq_major(xs[0], n_q), heads(kp), heads(vp), cp, fp,
            heads(to_seq_major(ks, n_q)), heads(to_seq_major(vs, n_q)), to_seq_major(cs, 2), to_seq_major(fs, 2))
```

```python
import functools

import jax
import jax.numpy as jnp
from jax import lax
from jax.experimental import pallas as pl
from jax.experimental.pallas import tpu as pltpu

D_MODEL = 1024
N_HEADS = 8
HEAD_DIM = 64
ATT_WIDTH = N_HEADS * HEAD_DIM
CONV_WIDTH = 512
BLOCK = 256
TOP_K = 3
N_BUCKETS = 32
MAX_DIST = 1024
D_FF = 2816
PAGE_SIZE = 128
EPS = 1e-6
NEG = -1e30
REMOVED = -3e38
LOG2E = 1.4426950408889634
N_IN = 3 * ATT_WIDTH + 3 * CONV_WIDTH + 2 * D_MODEL
LANES = 128
FFN_CHUNK = 256
N_BIAS_TILES = 6
MAX_BLOCKS = 32
V_ROWS = HEAD_DIM + 16
VMEM_LIMIT = 56 * 1024 * 1024

_BF = jnp.bfloat16
_F32 = jnp.float32


def _bucket_upper_bounds():
    max_exact = N_BUCKETS // 2
    n_log = N_BUCKETS - max_exact
    ratio = MAX_DIST // max_exact
    assert ratio * max_exact == MAX_DIST
    ups = [b + 1 for b in range(max_exact)]
    for k in range(1, n_log):
        target = (max_exact ** n_log) * (ratio ** k)
        n = max_exact
        while n ** n_log < target:
            n += 1
        ups.append(n)
    return ups


_BUCKET_UPPER = _bucket_upper_bounds()
assert (N_BIAS_TILES - 2) * BLOCK + 1 >= _BUCKET_UPPER[-1]


def _dot(a, b):
    return jnp.dot(a, b, preferred_element_type=_F32)


def _dot_nt(a, b):
    return lax.dot_general(a, b, (((1,), (1,)), ((), ())), preferred_element_type=_F32)


def _sigmoid(x):
    return 1.0 / (1.0 + jnp.exp(-x))


def _silu(x):
    return x * _sigmoid(x)


def _rmsnorm(x, g):
    return x * lax.rsqrt(jnp.mean(x * x, axis=-1, keepdims=True) + EPS) * g


def _mod_chunk(mod_ref, idx, reps):
    v = mod_ref[0, :, idx * D_MODEL:(idx + 1) * D_MODEL]
    if reps > 1:
        v = jnp.concatenate([v] * reps, axis=0)
    return v


def _params(sem, vmem=VMEM_LIMIT):
    return pltpu.CompilerParams(dimension_semantics=sem, vmem_limit_bytes=vmem)


def _const_spec(shape):
    nd = len(shape)
    return pl.BlockSpec(shape, lambda *_: (0,) * nd, pipeline_mode=pl.Buffered(1))


def _layer_spec(stacked, layer):
    return pl.BlockSpec((1,) + stacked.shape[1:], lambda *_: (layer, 0, 0), pipeline_mode=pl.Buffered(1))


def _mod_kernel(c_ref, w_ref, b_ref, o_ref):
    a = _silu(c_ref[...]).astype(_BF)
    o_ref[0] = _dot(a, w_ref[0].astype(_BF)) + b_ref[0]


def _modulation(c_all, ada_w, ada_b):
    depth = ada_w.shape[0]
    rows = c_all.shape[0]
    n_chunks = ada_w.shape[2] // D_MODEL
    return pl.pallas_call(
        _mod_kernel,
        grid=(depth, n_chunks),
        in_specs=[
            pl.BlockSpec((rows, D_MODEL), lambda l, n: (0, 0)),
            pl.BlockSpec((1, D_MODEL, D_MODEL), lambda l, n: (l, 0, n)),
            pl.BlockSpec((1, 1, D_MODEL), lambda l, n: (l, 0, n)),
        ],
        out_specs=pl.BlockSpec((1, rows, D_MODEL), lambda l, n: (l, 0, n)),
        out_shape=jax.ShapeDtypeStruct((depth, rows, ada_w.shape[2]), _F32),
        compiler_params=_params(("arbitrary", "arbitrary")),
        name="modulation",
    )(c_all, ada_w, ada_b.reshape(depth, 1, -1))


def _bias_kernel(rb_ref, bp_ref, bs_ref, *, past_len):
    h = pl.program_id(0)

    def table(dist, scale):
        n = jnp.maximum(dist, 0)
        v = jnp.full(dist.shape, rb_ref[N_BUCKETS - 1, h], _F32)
        for b in range(N_BUCKETS - 2, -1, -1):
            v = jnp.where(n < _BUCKET_UPPER[b], rb_ref[b, h], v)
        return jnp.where(dist >= 0, v * scale, NEG)

    kk = lax.broadcasted_iota(jnp.int32, (BLOCK, BLOCK), 0)
    qq = lax.broadcasted_iota(jnp.int32, (BLOCK, BLOCK), 1)
    for d in range(N_BIAS_TILES):
        bp_ref[0, d] = table(d * BLOCK + qq - kk, LOG2E)
    t = lax.broadcasted_iota(jnp.int32, bs_ref.shape[1:], 0)
    p = lax.broadcasted_iota(jnp.int32, bs_ref.shape[1:], 1)
    bs_ref[0] = table(past_len + t - p, 1.0)


def _bias_tables(rel_bias, past_len):
    s_cols = past_len + PAGE_SIZE
    return pl.pallas_call(
        functools.partial(_bias_kernel, past_len=past_len),
        grid=(N_HEADS,),
        in_specs=[pl.BlockSpec(memory_space=pltpu.SMEM)],
        out_specs=[
            pl.BlockSpec((1, N_BIAS_TILES, BLOCK, BLOCK), lambda h: (h, 0, 0, 0)),
            pl.BlockSpec((1, 8, s_cols), lambda h: (h, 0, 0)),
        ],
        out_shape=[
            jax.ShapeDtypeStruct((N_HEADS, N_BIAS_TILES, BLOCK, BLOCK), _F32),
            jax.ShapeDtypeStruct((N_HEADS, 8, s_cols), _F32),
        ],
        compiler_params=_params(("arbitrary",)),
        name="bias_tables",
    )(rel_bias)


def _modulated_norm(x, g, sh, sc):
    return (_rmsnorm(x, g) * (1.0 + sc) + sh).astype(_BF)


def _project(hb, w_ref):
    bounds = (0, ATT_WIDTH, 2 * ATT_WIDTH, 3 * ATT_WIDTH, 3 * ATT_WIDTH + CONV_WIDTH,
              3 * ATT_WIDTH + 2 * CONV_WIDTH, 3 * ATT_WIDTH + 3 * CONV_WIDTH,
              3 * ATT_WIDTH + 3 * CONV_WIDTH + D_MODEL, N_IN)
    return [_dot(hb, w_ref[0, :, a:b]) for a, b in zip(bounds[:-1], bounds[1:])]


def _inproj_prompt_kernel(x_ref, mod_ref, g_ref, w_ref, wkx_ref, cw_ref,
                          k_ref, v_ref, kx_ref, qtz_ref, qt_ref, vt_ref, kmean_ref, cb_ref, ga_ref, gb_ref,
                          cstate_ref, ext_ref):
    tr = x_ref.shape[1]
    t = pl.program_id(1)

    @pl.when(t == 0)
    def _():
        ext_ref[0:8, :] = jnp.zeros((8, CONV_WIDTH), _F32)

    hb = _modulated_norm(x_ref[0], g_ref[...], _mod_chunk(mod_ref, 0, 1), _mod_chunk(mod_ref, 1, 1))
    q, k, v, u, bg, cg, ga, gb = _project(hb, w_ref)

    qt = jnp.transpose(q * (HEAD_DIM ** -0.5))
    zeros = jnp.zeros((HEAD_DIM, tr), _F32)
    for h in range(N_HEADS):
        qh = qt[h * HEAD_DIM:(h + 1) * HEAD_DIM]
        pair = [qh, zeros] if h % 2 == 0 else [zeros, qh]
        qtz_ref[0, h] = jnp.concatenate(pair, axis=0).astype(_BF)
    qt_ref[0] = (qt * LOG2E).astype(_BF)

    k_ref[0] = k
    kmean_ref[0, 0] = jnp.mean(k, axis=0, keepdims=True)
    lane = lax.broadcasted_iota(jnp.int32, (tr, N_HEADS * LANES), 1) % LANES
    block_lanes = (lane == HEAD_DIM + t) | (lane == HEAD_DIM + MAX_BLOCKS + t)
    kx = jnp.where(block_lanes, 1.0, _dot(hb, wkx_ref[0])).astype(_BF)
    for h in range(N_HEADS):
        kx_ref[0, 0, h] = kx[:, h * LANES:(h + 1) * LANES]

    v_ref[0] = v
    vt = jnp.transpose(v)
    ones = jnp.ones((V_ROWS - HEAD_DIM, tr), _BF)
    for h in range(N_HEADS):
        vt_ref[0, h, 0] = jnp.concatenate([vt[h * HEAD_DIM:(h + 1) * HEAD_DIM].astype(_BF), ones], axis=0)

    ext_ref[8:8 + tr, :] = cg * u
    cw = cw_ref[...]
    conv = (cw[0:1] * ext_ref[6:6 + tr, :] + cw[1:2] * ext_ref[7:7 + tr, :]
            + cw[2:3] * ext_ref[8:8 + tr, :])
    cb_ref[0] = (bg * conv).astype(_BF)
    cstate_ref[0] = ext_ref[tr + 6:tr + 8, :]
    ext_ref[0:8, :] = ext_ref[tr:tr + 8, :]

    ga_ref[0] = _sigmoid(ga)
    gb_ref[0] = _sigmoid(gb)


def _inproj_prompt(x, mod, g, w_bf, wkx_bf, cw, layer):
    bsz, seq, _ = x.shape
    tr = BLOCK
    nt = seq // tr
    assert nt == MAX_BLOCKS
    row = lambda w: pl.BlockSpec((1, tr, w), lambda b, t: (b, t, 0))
    return pl.pallas_call(
        _inproj_prompt_kernel,
        grid=(bsz, nt),
        in_specs=[
            row(D_MODEL),
            pl.BlockSpec((1, 1, mod.shape[2]), lambda b, t: (b, 0, 0)),
            _const_spec((1, D_MODEL)),
            _layer_spec(w_bf, layer),
            _layer_spec(wkx_bf, layer),
            _const_spec((3, CONV_WIDTH)),
        ],
        out_specs=[
            row(ATT_WIDTH), row(ATT_WIDTH),
            pl.BlockSpec((1, 1, N_HEADS, tr, LANES), lambda b, t: (b, t, 0, 0, 0)),
            pl.BlockSpec((1, N_HEADS, 2 * HEAD_DIM, tr), lambda b, t: (b, 0, 0, t)),
            pl.BlockSpec((1, ATT_WIDTH, tr), lambda b, t: (b, 0, t)),
            pl.BlockSpec((1, N_HEADS, 1, V_ROWS, tr), lambda b, t: (b, 0, t, 0, 0)),
            pl.BlockSpec((1, 1, 1, ATT_WIDTH), lambda b, t: (b, t, 0, 0)),
            row(CONV_WIDTH), row(D_MODEL), row(D_MODEL),
            pl.BlockSpec((1, 2, CONV_WIDTH), lambda b, t: (b, 0, 0)),
        ],
        out_shape=[
            jax.ShapeDtypeStruct((bsz, seq, ATT_WIDTH), _F32),
            jax.ShapeDtypeStruct((bsz, seq, ATT_WIDTH), _F32),
            jax.ShapeDtypeStruct((bsz, nt, N_HEADS, tr, LANES), _BF),
            jax.ShapeDtypeStruct((bsz, N_HEADS, 2 * HEAD_DIM, seq), _BF),
            jax.ShapeDtypeStruct((bsz, ATT_WIDTH, seq), _BF),
            jax.ShapeDtypeStruct((bsz, N_HEADS, nt, V_ROWS, tr), _BF),
            jax.ShapeDtypeStruct((bsz, nt, 1, ATT_WIDTH), _F32),
            jax.ShapeDtypeStruct((bsz, seq, CONV_WIDTH), _BF),
            jax.ShapeDtypeStruct((bsz, seq, D_MODEL), _F32),
            jax.ShapeDtypeStruct((bsz, seq, D_MODEL), _F32),
            jax.ShapeDtypeStruct((bsz, 2, CONV_WIDTH), _F32),
        ],
        scratch_shapes=[pltpu.VMEM((tr + 8, CONV_WIDTH), _F32)],
        compiler_params=_params(("arbitrary", "arbitrary")),
        name="inproj_prompt",
    )(x, mod, g, w_bf, wkx_bf, cw)


def _inproj_sample_kernel(x_ref, mod_ref, g_ref, w_ref, cw_ref, prev_ref,
                          q_ref, k_ref, v_ref, cb_ref, ga_ref, gb_ref, cstate_ref, *, reps):
    rows = x_ref.shape[0]
    stride = rows // reps
    hb = _modulated_norm(x_ref[...], g_ref[...], _mod_chunk(mod_ref, 0, reps), _mod_chunk(mod_ref, 1, reps))
    q, k, v, u, bg, cg, ga, gb = _project(hb, w_ref)
    q_ref[...] = q * (HEAD_DIM ** -0.5)
    k_ref[...] = k
    v_ref[...] = v
    cu = cg * u
    ext = jnp.concatenate([prev_ref[...], cu], axis=0)
    cw = cw_ref[...]
    conv = (cw[0:1] * ext[0:rows] + cw[1:2] * ext[stride:stride + rows]
            + cw[2:3] * ext[2 * stride:2 * stride + rows])
    cb_ref[...] = (bg * conv).astype(_BF)
    cstate_ref[...] = ext[rows:rows + 2 * stride]
    ga_ref[...] = _sigmoid(ga)
    gb_ref[...] = _sigmoid(gb)


def _inproj_sample(x, mod, g, w_bf, cw, prev, reps, layer):
    rows = x.shape[0]
    full = lambda a: pl.BlockSpec(a.shape, lambda i: (0,) * a.ndim)
    shp = lambda w, dt=_F32: jax.ShapeDtypeStruct((rows, w), dt)
    outs = [shp(ATT_WIDTH), shp(ATT_WIDTH), shp(ATT_WIDTH), shp(CONV_WIDTH, _BF),
            shp(D_MODEL), shp(D_MODEL), jax.ShapeDtypeStruct(prev.shape, _F32)]
    return pl.pallas_call(
        functools.partial(_inproj_sample_kernel, reps=reps),
        grid=(1,),
        in_specs=[full(x), full(mod), full(g), _layer_spec(w_bf, layer), full(cw), full(prev)],
        out_specs=[pl.BlockSpec(o.shape, lambda i: (0, 0)) for o in outs],
        out_shape=outs,
        compiler_params=_params(("arbitrary",)),
        name="inproj_sample",
    )(x, mod, g, w_bf, cw, prev)


def _top_k_rows(gate, own):
    n = gate.shape[0]
    row = lax.broadcasted_iota(jnp.int32, gate.shape, 0).astype(_F32)
    past = row < own
    gate = jnp.where(past, gate, NEG)
    mask = jnp.where(row == own, 0.0, NEG)
    for _ in range(min(TOP_K, n)):
        mx = jnp.max(gate, axis=0, keepdims=True)
        first = jnp.min(jnp.where(gate == mx, row, float(n)), axis=0, keepdims=True)
        chosen = row == first
        mask = jnp.where(chosen, jnp.where(past, 0.0, mask), mask)
        gate = jnp.where(chosen, REMOVED, gate)
    return mask


def _attn_prompt_kernel(qtz_ref, qt_ref, kx_ref, vt_ref, kmean_ref, bias_ref, o_ref,
                        qx_ref, m_ref, acc_ref, s_ref):
    i = pl.program_id(1)
    tq = qt_ref.shape[2]
    own = i.astype(_F32)
    n_far = jnp.maximum(i - (N_BIAS_TILES - 2), 0)
    is_far = lax.broadcasted_iota(jnp.int32, (MAX_BLOCKS, tq), 0) < n_far
    for h in range(N_HEADS):
        km = kmean_ref[0, :, (h // 2) * LANES:(h // 2 + 1) * LANES].astype(_BF)
        mask = _top_k_rows(_dot(km, qtz_ref[0, h]), own)
        far_bias = bias_ref[h, N_BIAS_TILES - 1, 0:1, 0:1]
        ext = mask + jnp.where(is_far, far_bias, 0.0)
        hi = ext.astype(_BF)
        qx_ref[h, 0:HEAD_DIM] = qt_ref[0, h * HEAD_DIM:(h + 1) * HEAD_DIM, :]
        qx_ref[h, HEAD_DIM:HEAD_DIM + MAX_BLOCKS] = hi
        qx_ref[h, HEAD_DIM + MAX_BLOCKS:] = (ext - hi.astype(_F32)).astype(_BF)
    m_ref[...] = jnp.full(m_ref.shape, NEG, _F32)
    acc_ref[...] = jnp.zeros(acc_ref.shape, _F32)

    n_pairs = N_HEADS // 2
    last_block = kx_ref.shape[1] - 1

    def logits(j, pair):
        return jnp.concatenate(
            [_dot(kx_ref[0, j, 2 * pair + e], qx_ref[2 * pair + e]) for e in range(2)], axis=1)

    ahead = s_ref.shape[0]
    for pair in range(ahead):
        s_ref[pair] = logits(0, pair)

    def body(j, carry, near):
        pending = [s_ref[pair] for pair in range(ahead)]
        for pair in range(n_pairs):
            s2 = pending.pop(0)
            if pair + ahead < n_pairs:
                pending.append(logits(j, pair + ahead))
            else:
                s_ref[pair + ahead - n_pairs] = logits(jnp.minimum(j + 1, last_block), pair + ahead - n_pairs)
            for e in range(2):
                h = 2 * pair + e
                s = s2[:, e * tq:(e + 1) * tq]
                if near:
                    s = s + bias_ref[h, i - j]
                m_old = m_ref[h:h + 1, :]
                m_new = jnp.maximum(m_old, jnp.max(s, axis=0, keepdims=True))
                alpha = jnp.exp2(m_old - m_new)
                p = jnp.exp2(s - m_new)
                m_ref[h:h + 1, :] = m_new
                acc_ref[h] = alpha * acc_ref[h] + _dot(vt_ref[0, h, j], p.astype(_BF))
        return carry

    def run(lo, hi, near):
        n = hi - lo

        def four_blocks(jj, carry):
            for u in range(4):
                body(lo + 4 * jj + u, carry, near)
            return carry

        lax.fori_loop(0, n // 4, four_blocks, 0)

        @pl.when(n % 4 >= 2)
        def _():
            body(lo + (n // 4) * 4, 0, near)
            body(lo + (n // 4) * 4 + 1, 0, near)

        @pl.when(n % 2 == 1)
        def _():
            body(hi - 1, 0, near)

    run(0, n_far, False)
    run(n_far, i + 1, True)
    for h in range(N_HEADS):
        acc = acc_ref[h]
        o_ref[0, h * HEAD_DIM:(h + 1) * HEAD_DIM, :] = (acc[:HEAD_DIM] / acc[HEAD_DIM:HEAD_DIM + 1]).astype(_BF)


def _attn_prompt(qtz, qt, kx, vt, kmean, bias_p):
    bsz, _, seq = qt.shape
    nb = seq // BLOCK
    km3 = kmean.reshape(bsz, nb, ATT_WIDTH)
    per_batch = lambda shape: pl.BlockSpec(
        (1,) + shape[1:], lambda b, i: (b,) + (0,) * (len(shape) - 1), pipeline_mode=pl.Buffered(1))
    return pl.pallas_call(
        _attn_prompt_kernel,
        grid=(bsz, nb),
        in_specs=[
            pl.BlockSpec((1, N_HEADS, 2 * HEAD_DIM, BLOCK), lambda b, i: (b, 0, 0, i)),
            pl.BlockSpec((1, ATT_WIDTH, BLOCK), lambda b, i: (b, 0, i)),
            per_batch(kx.shape), per_batch(vt.shape), per_batch(km3.shape),
            _const_spec(bias_p.shape),
        ],
        out_specs=pl.BlockSpec((1, ATT_WIDTH, BLOCK), lambda b, i: (b, 0, i)),
        out_shape=jax.ShapeDtypeStruct((bsz, ATT_WIDTH, seq), _BF),
        scratch_shapes=[
            pltpu.VMEM((N_HEADS, LANES, BLOCK), _BF),
            pltpu.VMEM((N_HEADS, BLOCK), _F32),
            pltpu.VMEM((N_HEADS, V_ROWS, BLOCK), _F32),
            pltpu.VMEM((2, BLOCK, 2 * BLOCK), _F32),
        ],
        compiler_params=_params(("arbitrary", "arbitrary")),
        name="attn_prompt",
    )(qtz, qt, kx, vt, km3, bias_p)


def _attn_sample_kernel(pt_ref, q_ref, kn_ref, vn_ref, bias_ref, *refs, n_pages, n_q, group):
    del pt_ref
    o_ref = refs[2 * group * n_pages]
    for s in range(group):
        k_refs = refs[s * n_pages:(s + 1) * n_pages]
        v_refs = refs[(group + s) * n_pages:(group + s + 1) * n_pages]
        o_ref[:, s, 0, :] = _attend_sample_seq(
            q_ref[:, s, 0, :], kn_ref[:, s, 0, :], vn_ref[:, s, 0, :], bias_ref, k_refs, v_refs, n_q)


def _rows_to_page(x, n_rows):
    row = lax.broadcasted_iota(jnp.int32, (PAGE_SIZE, x.shape[1]), 0)
    out = jnp.zeros((PAGE_SIZE, x.shape[1]), _F32)
    for t in range(n_rows):
        out = jnp.where(row == t, x[t:t + 1], out)
    return out.astype(_BF)


def _attend_sample_seq(q, k_new, v_new, bias_ref, k_refs, v_refs, n_q):
    n_pages = len(k_refs)
    rows = n_q * N_HEADS
    pages_per_block = BLOCK // PAGE_SIZE
    n_blocks = n_pages // pages_per_block

    rep = jnp.concatenate([jnp.broadcast_to(q[t:t + 1], (N_HEADS, ATT_WIDTH)) for t in range(n_q)], axis=0)
    r_head = lax.broadcasted_iota(jnp.int32, (rows, ATT_WIDTH), 0) % N_HEADS
    c_head = lax.broadcasted_iota(jnp.int32, (rows, ATT_WIDTH), 1) // HEAD_DIM
    head_lanes = r_head == c_head
    q_rows = jnp.where(head_lanes, rep, 0.0).astype(_BF)

    kt_pages = [r[0, 0] for r in k_refs]
    block_sums = [
        jnp.sum(sum(kt_pages[n * pages_per_block:(n + 1) * pages_per_block]), axis=1, keepdims=True)
        for n in range(n_blocks)]
    kmean_t = jnp.concatenate(block_sums, axis=1) * (1.0 / BLOCK)
    gate = _dot(q_rows, kmean_t.astype(_BF))

    col = lax.broadcasted_iota(jnp.int32, gate.shape, 1).astype(_F32)
    sel = jnp.full(gate.shape, NEG, _F32)
    for _ in range(min(TOP_K, n_blocks)):
        mx = jnp.max(gate, axis=1, keepdims=True)
        first = jnp.min(jnp.where(gate == mx, col, float(n_blocks)), axis=1, keepdims=True)
        chosen = col == first
        sel = jnp.where(chosen, 0.0, sel)
        gate = jnp.where(chosen, REMOVED, gate)

    logits = []
    for p in range(n_pages):
        s = _dot(q_rows, kt_pages[p].astype(_BF)) + bias_ref[:, p * PAGE_SIZE:(p + 1) * PAGE_SIZE]
        n = p // pages_per_block
        logits.append(s + sel[:, n:n + 1])
    k_own = _rows_to_page(k_new, n_q)
    v_own = _rows_to_page(v_new, n_q)
    logits.append(_dot_nt(q_rows, k_own) + bias_ref[:, n_pages * PAGE_SIZE:])

    m = functools.reduce(jnp.maximum, logits)
    m = jnp.max(m, axis=1, keepdims=True)
    l = jnp.zeros((rows, 1), _F32)
    acc = jnp.zeros((rows, ATT_WIDTH), _F32)
    for p in range(n_pages + 1):
        pr = jnp.exp(logits[p] - m)
        l = l + jnp.sum(pr, axis=1, keepdims=True)
        if p == n_pages:
            acc = acc + _dot(pr.astype(_BF), v_own)
        else:
            acc = acc + _dot_nt(pr.astype(_BF), v_refs[p][0, 0].astype(_BF))
    out = jnp.where(head_lanes, acc / l, 0.0)
    return jnp.sum(out.reshape(n_q, N_HEADS, ATT_WIDTH), axis=1)


def _attn_sample(page_table, q, k_new, v_new, bias_s, cache_kt, cache_vt, layer):
    n_seq, n_pages = page_table.shape
    n_q = q.shape[0]
    group = 2
    assert n_seq % group == 0
    page_spec = lambda s, p: pl.BlockSpec(
        (1, 1, ATT_WIDTH, PAGE_SIZE), lambda g, pt: (layer, pt[group * g + s, p], 0, 0))
    pages = [page_spec(s, p) for s in range(group) for p in range(n_pages)]
    per_group = pl.BlockSpec((n_q, group, 1, ATT_WIDTH), lambda g, pt: (0, g, 0, 0))
    grid_spec = pltpu.PrefetchScalarGridSpec(
        num_scalar_prefetch=1,
        grid=(n_seq // group,),
        in_specs=[per_group, per_group, per_group,
                  pl.BlockSpec(bias_s.shape, lambda g, pt: (0, 0), pipeline_mode=pl.Buffered(1))]
                 + pages * 2,
        out_specs=per_group,
    )
    n_in = group * n_pages
    return pl.pallas_call(
        functools.partial(_attn_sample_kernel, n_pages=n_pages, n_q=n_q, group=group),
        grid_spec=grid_spec,
        out_shape=jax.ShapeDtypeStruct(q.shape, _F32),
        compiler_params=_params(("arbitrary",)),
        name="attn_sample",
    )(page_table, q, k_new, v_new, bias_s, *([cache_kt] * n_in), *([cache_vt] * n_in))


def _post_kernel(x_ref, att_ref, cb_ref, ga_ref, gb_ref, mod_ref, g_ref, wa_ref, wc_ref, wo_ref,
                 x1_ref, h2_ref, *, reps, att_transposed):
    rows = x_ref.shape[1]
    part = min(rows, BLOCK)
    parts = [slice(r, r + part) for r in range(0, rows, part)]

    def mod_rows(idx, sl):
        m = _mod_chunk(mod_ref, idx, reps)
        return m if m.shape[0] == 1 else m[sl]

    branches = []
    for sl in parts:
        if att_transposed:
            att = jnp.transpose(att_ref[0, :, sl].astype(_F32)).astype(_BF)
        else:
            att = att_ref[0, sl].astype(_BF)
        branches.append((_dot(att, wa_ref[0]), _dot(cb_ref[0, sl], wc_ref[0])))
    mixed = []
    for sl, (y_a, y_b) in zip(parts, branches):
        merged = ga_ref[0, sl] * y_a + gb_ref[0, sl] * y_b
        mixed.append(_dot(merged.astype(_BF), wo_ref[0]))
    for sl, o in zip(parts, mixed):
        x1 = x_ref[0, sl] + mod_rows(2, sl) * o
        x1_ref[0, sl] = x1
        h2 = _rmsnorm(x1, g_ref[...]) * (1.0 + mod_rows(4, sl)) + mod_rows(3, sl)
        h2_ref[0, sl] = h2.astype(_BF)


def _post(x, att, cb, ga, gb, mod, g, wa, wc, wo, *, layer, tr, att_transposed):
    bsz, seq, _ = x.shape
    nt = seq // tr
    reps = 1 if mod.shape[1] == 1 else tr // mod.shape[1]
    row = lambda w: pl.BlockSpec((1, tr, w), lambda b, t: (b, t, 0))
    att_spec = (pl.BlockSpec((1, ATT_WIDTH, tr), lambda b, t: (b, 0, t)) if att_transposed
                else row(ATT_WIDTH))
    return pl.pallas_call(
        functools.partial(_post_kernel, reps=reps, att_transposed=att_transposed),
        grid=(bsz, nt),
        in_specs=[
            row(D_MODEL), att_spec, row(CONV_WIDTH), row(D_MODEL), row(D_MODEL),
            pl.BlockSpec((1,) + mod.shape[1:], lambda b, t: (b, 0, 0)),
            _const_spec((1, D_MODEL)),
            _layer_spec(wa, layer), _layer_spec(wc, layer), _layer_spec(wo, layer),
        ],
        out_specs=[row(D_MODEL), row(D_MODEL)],
        out_shape=[jax.ShapeDtypeStruct(x.shape, _F32), jax.ShapeDtypeStruct(x.shape, _BF)],
        compiler_params=_params(("arbitrary", "arbitrary")),
        name="post_attn",
    )(x, att, cb, ga, gb, mod, g, wa, wc, wo)


def _ffn_kernel(*refs, reps, stride, final):
    if stride == 1:
        x1_ref, h2_ref, mod_ref, wu_ref, cw_ref, cbias_ref, wd_ref = refs[:7]
        rest = refs[7:]
        prev_ref = None
    else:
        x1_ref, h2_ref, mod_ref, wu_ref, cw_ref, cbias_ref, wd_ref, prev_ref = refs[:8]
        rest = refs[8:]
    if final:
        fg_ref, rest = rest[0], rest[1:]
    out_ref, state_ref = rest[0], rest[1]
    ext_ref = rest[2] if stride == 1 else None

    rows = x1_ref.shape[1]
    hb = h2_ref[0]
    cw = cw_ref[...]
    cbias = cbias_ref[...]

    if stride == 1:
        @pl.when(pl.program_id(1) == 0)
        def _():
            ext_ref[0:8, :] = jnp.zeros((8, 2 * D_FF), _F32)

    n_chunks = D_FF // FFN_CHUNK
    a_cols = lambda c: slice(c * FFN_CHUNK, (c + 1) * FFN_CHUNK)
    b_cols = lambda c: slice(D_FF + c * FFN_CHUNK, D_FF + (c + 1) * FFN_CHUNK)
    up_pair = lambda c: (_dot(hb, wu_ref[0, :, a_cols(c)]), _dot(hb, wu_ref[0, :, b_cols(c)]))

    def conv_cols(up, sl):
        if stride == 1:
            ext_ref[8:8 + rows, sl] = up
            e0, e1, e2 = ext_ref[6:6 + rows, sl], ext_ref[7:7 + rows, sl], up
        else:
            ext = jnp.concatenate([prev_ref[:, sl], up], axis=0)
            e0, e1, e2 = ext[0:rows], ext[stride:stride + rows], up
            state_ref[:, sl] = ext[rows:rows + 2 * stride]
        return cw[0:1, sl] * e0 + cw[1:2, sl] * e1 + cw[2:3, sl] * e2 + cbias[:, sl]

    acc = jnp.zeros((rows, D_MODEL), _F32)
    ahead = 4
    pending = [up_pair(c) for c in range(ahead)]
    for c in range(n_chunks):
        up_a, up_b = pending.pop(0)
        if c + ahead < n_chunks:
            pending.append(up_pair(c + ahead))
        a = conv_cols(up_a, a_cols(c))
        b = conv_cols(up_b, b_cols(c))
        acc = acc + _dot((_silu(a) * b).astype(_BF), wd_ref[0, a_cols(c), :])

    if stride == 1:
        state_ref[0] = ext_ref[rows + 6:rows + 8, :]
        ext_ref[0:8, :] = ext_ref[rows:rows + 8, :]

    x2 = x1_ref[0] + _mod_chunk(mod_ref, 5, reps) * acc
    out_ref[0] = _rmsnorm(x2, fg_ref[...]) if final else x2


def _ffn(x1, h2, mod, wu, cw, cbias, wd, prev, final_g, *, layer, tr, stride):
    bsz, seq, _ = x1.shape
    nt = seq // tr
    reps = 1 if mod.shape[1] == 1 else tr // mod.shape[1]
    final = final_g is not None
    row = lambda w: pl.BlockSpec((1, tr, w), lambda b, t: (b, t, 0))
    in_specs = [row(D_MODEL), row(D_MODEL),
                pl.BlockSpec((1,) + mod.shape[1:], lambda b, t: (b, 0, 0)),
                _layer_spec(wu, layer), _const_spec(cw.shape), _const_spec(cbias.shape), _layer_spec(wd, layer)]
    args = [x1, h2, mod, wu, cw, cbias, wd]
    if stride == 1:
        state_shape = jax.ShapeDtypeStruct((bsz, 2, 2 * D_FF), _F32)
        state_spec = pl.BlockSpec((1, 2, 2 * D_FF), lambda b, t: (b, 0, 0))
        scratch = [pltpu.VMEM((tr + 8, 2 * D_FF), _F32)]
    else:
        assert bsz == 1 and nt == 1 and tr % stride == 0
        in_specs.append(_const_spec(prev.shape))
        args.append(prev)
        state_shape = jax.ShapeDtypeStruct(prev.shape, _F32)
        state_spec = pl.BlockSpec(prev.shape, lambda b, t: (0, 0))
        scratch = []
    if final:
        in_specs.append(_const_spec(final_g.shape))
        args.append(final_g)
    return pl.pallas_call(
        functools.partial(_ffn_kernel, reps=reps, stride=stride, final=final),
        grid=(bsz, nt),
        in_specs=in_specs,
        out_specs=[row(D_MODEL), state_spec],
        out_shape=[jax.ShapeDtypeStruct(x1.shape, _F32), state_shape],
        scratch_shapes=scratch,
        compiler_params=_params(("arbitrary", "arbitrary")),
        name="conv_ffn",
    )(*args)


def kernel(x_prompt, x_sample, c_prompt, c_sample, cache_k, cache_v, state_conv, state_ffn, page_table,
           rel_bias, ada_w, ada_b, norm_mix_g, norm_ffn_g, final_norm_g, w_in, conv_w, w_att_out,
           w_conv_out, w_o, w_up, ffn_conv_w, ffn_conv_b, w_down):
    depth = w_in.shape[0]
    bsz, seq, _ = x_prompt.shape
    n_seq, n_q, _ = x_sample.shape
    n_pages = page_table.shape[1]
    past_len = n_pages * PAGE_SIZE
    assert seq % BLOCK == 0 and past_len % BLOCK == 0 and n_q <= 8 and bsz <= 8
    assert cache_k.shape[2] == PAGE_SIZE

    pad_rows = 8
    c_all = jnp.concatenate([c_prompt, jnp.zeros((pad_rows - bsz, D_MODEL), _F32), c_sample], axis=0)
    mod_all = _modulation(c_all, ada_w, ada_b)
    mod_p = mod_all[:, :bsz].reshape(depth, bsz, 1, -1)
    mod_s = mod_all[:, pad_rows:].reshape(depth, 1, n_seq, -1)

    bias_p, bias_s8 = _bias_tables(rel_bias, past_len)
    bias_s = bias_s8[:, :n_q].transpose(1, 0, 2).reshape(n_q * N_HEADS, -1)

    page_t = lambda c: c.transpose(0, 1, 3, 4, 2).reshape(c.shape[0], c.shape[1], ATT_WIDTH, PAGE_SIZE)
    cache_kt, cache_vt = page_t(cache_k), page_t(cache_v)

    time_major = lambda a: a.transpose(1, 0, 2).reshape(-1, a.shape[2])
    seq_major = lambda a, steps: a.reshape(steps, n_seq, -1).transpose(1, 0, 2)
    rows_s = n_q * n_seq

    w_in_bf = w_in.astype(_BF)
    wk = w_in[:, :, ATT_WIDTH:2 * ATT_WIDTH].reshape(depth, D_MODEL, N_HEADS, HEAD_DIM)
    wkx_bf = jnp.pad(wk, ((0, 0), (0, 0), (0, 0), (0, LANES - HEAD_DIM))).reshape(depth, D_MODEL, -1).astype(_BF)
    wa, wc, wo = w_att_out.astype(_BF), w_conv_out.astype(_BF), w_o.astype(_BF)
    wu, wd = w_up.astype(_BF), w_down.astype(_BF)
    conv_prev = state_conv.transpose(0, 2, 1, 3).reshape(depth, 2 * n_seq, -1)
    ffn_prev = state_ffn.transpose(0, 2, 1, 3).reshape(depth, 2 * n_seq, -1)
    by_seq = lambda a: a.reshape(n_q, n_seq, 1, -1)

    xp = x_prompt
    xs = time_major(x_sample)[None]
    fin = final_norm_g.reshape(1, D_MODEL)
    outs = [[] for _ in range(8)]
    for l in range(depth):
        g_mix = norm_mix_g[l].reshape(1, D_MODEL)
        g_ffn = norm_ffn_g[l].reshape(1, D_MODEL)
        cbias = ffn_conv_b[l].reshape(1, -1)
        final_g = fin if l == depth - 1 else None

        k, v, kx, qtz, qt, vt, kmean, cb, ga, gb, cstate = _inproj_prompt(
            xp, mod_p[l], g_mix, w_in_bf, wkx_bf, conv_w[l], l)
        att_t = _attn_prompt(qtz, qt, kx, vt, kmean, bias_p)
        x1, h2 = _post(xp, att_t, cb, ga, gb, mod_p[l], g_ffn, wa, wc, wo,
                       layer=l, tr=2 * BLOCK, att_transposed=True)
        xp, fstate = _ffn(x1, h2, mod_p[l], wu, ffn_conv_w[l], cbias, wd, None, final_g,
                          layer=l, tr=BLOCK, stride=1)
        for dst, val in zip(outs[:4], (k, v, cstate, fstate)):
            dst.append(val)

        q_s, k_s, v_s, cb_s, ga_s, gb_s, cstate_s = _inproj_sample(
            xs[0], mod_s[l], g_mix, w_in_bf, conv_w[l], conv_prev[l], n_q, l)
        att_s = _attn_sample(page_table, by_seq(q_s), by_seq(k_s), by_seq(v_s), bias_s, cache_kt, cache_vt, l)
        x1s, h2s = _post(xs, att_s.reshape(1, rows_s, ATT_WIDTH), cb_s[None], ga_s[None], gb_s[None],
                         mod_s[l], g_ffn, wa, wc, wo, layer=l, tr=rows_s, att_transposed=False)
        xs, fstate_s = _ffn(x1s, h2s, mod_s[l], wu, ffn_conv_w[l], cbias, wd, ffn_prev[l], final_g,
                            layer=l, tr=rows_s, stride=n_seq)
        for dst, val in zip(outs[4:], (k_s, v_s, cstate_s, fstate_s)):
            dst.append(val)

    kp, vp, cp, fp, ks, vs, cs, fs = (jnp.stack(o) for o in outs)
    heads = lambda a: a.reshape(a.shape[:-1] + (N_HEADS, HEAD_DIM))
    to_seq_major = lambda a, steps: a.reshape(depth, steps, n_seq, -1).transpose(0, 2, 1, 3)
    return (xp, seq_major(xs[0], n_q), heads(kp), heads(vp), cp, fp,
            heads(to_seq_major(ks, n_q)), heads(to_seq_major(vs, n_q)), to_seq_major(cs, 2), to_seq_major(fs, 2))
```

```python
import functools

import jax
import jax.numpy as jnp
from jax import lax
from jax.experimental import pallas as pl
from jax.experimental.pallas import tpu as pltpu

D_MODEL = 1024
N_HEADS = 8
HEAD_DIM = 64
ATT_WIDTH = N_HEADS * HEAD_DIM
CONV_WIDTH = 512
BLOCK = 256
TOP_K = 3
N_BUCKETS = 32
MAX_DIST = 1024
D_FF = 2816
PAGE_SIZE = 128
EPS = 1e-6
NEG = -1e30
REMOVED = -3e38
LOG2E = 1.4426950408889634
N_IN = 3 * ATT_WIDTH + 3 * CONV_WIDTH + 2 * D_MODEL
LANES = 128
FFN_CHUNK = 256
N_BIAS_TILES = 6
MAX_BLOCKS = 32
V_ROWS = HEAD_DIM + 16
VMEM_LIMIT = 56 * 1024 * 1024

_BF = jnp.bfloat16
_F32 = jnp.float32


def _bucket_upper_bounds():
    max_exact = N_BUCKETS // 2
    n_log = N_BUCKETS - max_exact
    ratio = MAX_DIST // max_exact
    assert ratio * max_exact == MAX_DIST
    ups = [b + 1 for b in range(max_exact)]
    for k in range(1, n_log):
        target = (max_exact ** n_log) * (ratio ** k)
        n = max_exact
        while n ** n_log < target:
            n += 1
        ups.append(n)
    return ups


_BUCKET_UPPER = _bucket_upper_bounds()
assert (N_BIAS_TILES - 2) * BLOCK + 1 >= _BUCKET_UPPER[-1]


def _dot(a, b):
    return jnp.dot(a, b, preferred_element_type=_F32)


def _dot_nt(a, b):
    return lax.dot_general(a, b, (((1,), (1,)), ((), ())), preferred_element_type=_F32)


def _sigmoid(x):
    return 1.0 / (1.0 + jnp.exp(-x))


def _silu(x):
    return x * _sigmoid(x)


def _rmsnorm(x, g):
    return x * lax.rsqrt(jnp.mean(x * x, axis=-1, keepdims=True) + EPS) * g


def _mod_chunk(mod_ref, idx, reps):
    v = mod_ref[0, :, idx * D_MODEL:(idx + 1) * D_MODEL]
    if reps > 1:
        v = jnp.concatenate([v] * reps, axis=0)
    return v


def _params(sem, vmem=VMEM_LIMIT):
    return pltpu.CompilerParams(dimension_semantics=sem, vmem_limit_bytes=vmem)


def _const_spec(shape):
    nd = len(shape)
    return pl.BlockSpec(shape, lambda *_: (0,) * nd, pipeline_mode=pl.Buffered(1))


def _layer_spec(stacked, layer):
    return pl.BlockSpec((1,) + stacked.shape[1:], lambda *_: (layer, 0, 0), pipeline_mode=pl.Buffered(1))


def _mod_kernel(c_ref, w_ref, b_ref, o_ref):
    a = _silu(c_ref[...]).astype(_BF)
    o_ref[0] = _dot(a, w_ref[0].astype(_BF)) + b_ref[0]


def _modulation(c_all, ada_w, ada_b):
    depth = ada_w.shape[0]
    rows = c_all.shape[0]
    n_chunks = ada_w.shape[2] // D_MODEL
    return pl.pallas_call(
        _mod_kernel,
        grid=(depth, n_chunks),
        in_specs=[
            pl.BlockSpec((rows, D_MODEL), lambda l, n: (0, 0)),
            pl.BlockSpec((1, D_MODEL, D_MODEL), lambda l, n: (l, 0, n)),
            pl.BlockSpec((1, 1, D_MODEL), lambda l, n: (l, 0, n)),
        ],
        out_specs=pl.BlockSpec((1, rows, D_MODEL), lambda l, n: (l, 0, n)),
        out_shape=jax.ShapeDtypeStruct((depth, rows, ada_w.shape[2]), _F32),
        compiler_params=_params(("arbitrary", "arbitrary")),
        name="modulation",
    )(c_all, ada_w, ada_b.reshape(depth, 1, -1))


def _bias_kernel(rb_ref, bp_ref, bs_ref, *, past_len):
    h = pl.program_id(0)

    def table(dist, scale):
        n = jnp.maximum(dist, 0)
        v = jnp.full(dist.shape, rb_ref[N_BUCKETS - 1, h], _F32)
        for b in range(N_BUCKETS - 2, -1, -1):
            v = jnp.where(n < _BUCKET_UPPER[b], rb_ref[b, h], v)
        return jnp.where(dist >= 0, v * scale, NEG)

    kk = lax.broadcasted_iota(jnp.int32, (BLOCK, BLOCK), 0)
    qq = lax.broadcasted_iota(jnp.int32, (BLOCK, BLOCK), 1)
    for d in range(N_BIAS_TILES):
        bp_ref[0, d] = table(d * BLOCK + qq - kk, LOG2E)
    t = lax.broadcasted_iota(jnp.int32, bs_ref.shape[1:], 0)
    p = lax.broadcasted_iota(jnp.int32, bs_ref.shape[1:], 1)
    bs_ref[0] = table(past_len + t - p, 1.0)


def _bias_tables(rel_bias, past_len):
    s_cols = past_len + PAGE_SIZE
    return pl.pallas_call(
        functools.partial(_bias_kernel, past_len=past_len),
        grid=(N_HEADS,),
        in_specs=[pl.BlockSpec(memory_space=pltpu.SMEM)],
        out_specs=[
            pl.BlockSpec((1, N_BIAS_TILES, BLOCK, BLOCK), lambda h: (h, 0, 0, 0)),
            pl.BlockSpec((1, 8, s_cols), lambda h: (h, 0, 0)),
        ],
        out_shape=[
            jax.ShapeDtypeStruct((N_HEADS, N_BIAS_TILES, BLOCK, BLOCK), _F32),
            jax.ShapeDtypeStruct((N_HEADS, 8, s_cols), _F32),
        ],
        compiler_params=_params(("arbitrary",)),
        name="bias_tables",
    )(rel_bias)


def _modulated_norm(x, g, sh, sc):
    return (_rmsnorm(x, g) * (1.0 + sc) + sh).astype(_BF)


_GROUP_BOUNDS = (0, ATT_WIDTH, 2 * ATT_WIDTH, 3 * ATT_WIDTH, 3 * ATT_WIDTH + CONV_WIDTH,
                 3 * ATT_WIDTH + 2 * CONV_WIDTH, 3 * ATT_WIDTH + 3 * CONV_WIDTH,
                 3 * ATT_WIDTH + 3 * CONV_WIDTH + D_MODEL, N_IN)


def _project(hb, w_ref):
    return [_dot(hb, w_ref[0, :, a:b]) for a, b in zip(_GROUP_BOUNDS[:-1], _GROUP_BOUNDS[1:])]


def _inproj_prompt_kernel(x_ref, mod_ref, g_ref, w_ref, cw_ref,
                          k_ref, v_ref, kx_ref, qtz_ref, qt_ref, vt_ref, kmean_ref, cb_ref, ga_ref, gb_ref,
                          cstate_ref, ext_ref):
    tr = x_ref.shape[1]
    t = pl.program_id(1)

    @pl.when(t == 0)
    def _():
        ext_ref[0:8, :] = jnp.zeros((8, CONV_WIDTH), _F32)

    hb = _modulated_norm(x_ref[0], g_ref[...], _mod_chunk(mod_ref, 0, 1), _mod_chunk(mod_ref, 1, 1))
    q, k, v, u, bg, cg, ga, gb = _project(hb, w_ref)

    qt = jnp.transpose(q * (HEAD_DIM ** -0.5))
    zeros = jnp.zeros((HEAD_DIM, tr), _F32)
    for h in range(N_HEADS):
        qh = qt[h * HEAD_DIM:(h + 1) * HEAD_DIM]
        pair = [qh, zeros] if h % 2 == 0 else [zeros, qh]
        qtz_ref[0, h] = jnp.concatenate(pair, axis=0).astype(_BF)
    qt_ref[0] = (qt * LOG2E).astype(_BF)

    k_ref[0] = k
    kmean_ref[0, 0] = jnp.mean(k, axis=0, keepdims=True)
    lane = lax.broadcasted_iota(jnp.int32, (tr, LANES), 1)
    upper = jnp.where((lane == HEAD_DIM + t) | (lane == HEAD_DIM + MAX_BLOCKS + t), 1.0, 0.0)
    for pair in range(N_HEADS // 2):
        slab = k[:, pair * LANES:(pair + 1) * LANES]
        for e, k_low in enumerate((slab, pltpu.roll(slab, HEAD_DIM, axis=1))):
            kx_ref[0, 0, 2 * pair + e] = jnp.where(lane < HEAD_DIM, k_low, upper).astype(_BF)

    v_ref[0] = v
    vt = jnp.transpose(v)
    ones = jnp.ones((V_ROWS - HEAD_DIM, tr), _BF)
    for h in range(N_HEADS):
        vt_ref[0, h, 0] = jnp.concatenate([vt[h * HEAD_DIM:(h + 1) * HEAD_DIM].astype(_BF), ones], axis=0)

    ext_ref[8:8 + tr, :] = cg * u
    cw = cw_ref[...]
    conv = (cw[0:1] * ext_ref[6:6 + tr, :] + cw[1:2] * ext_ref[7:7 + tr, :]
            + cw[2:3] * ext_ref[8:8 + tr, :])
    cb_ref[0] = (bg * conv).astype(_BF)
    cstate_ref[0] = ext_ref[tr + 6:tr + 8, :]
    ext_ref[0:8, :] = ext_ref[tr:tr + 8, :]

    ga_ref[0] = _sigmoid(ga)
    gb_ref[0] = _sigmoid(gb)


def _inproj_prompt(x, mod, g, w_bf, cw, layer):
    bsz, seq, _ = x.shape
    tr = BLOCK
    nt = seq // tr
    assert nt == MAX_BLOCKS
    row = lambda w: pl.BlockSpec((1, tr, w), lambda b, t: (b, t, 0))
    return pl.pallas_call(
        _inproj_prompt_kernel,
        grid=(bsz, nt),
        in_specs=[
            row(D_MODEL),
            pl.BlockSpec((1, 1, mod.shape[2]), lambda b, t: (b, 0, 0)),
            _const_spec((1, D_MODEL)),
            _layer_spec(w_bf, layer),
            _const_spec((3, CONV_WIDTH)),
        ],
        out_specs=[
            row(ATT_WIDTH), row(ATT_WIDTH),
            pl.BlockSpec((1, 1, N_HEADS, tr, LANES), lambda b, t: (b, t, 0, 0, 0)),
            pl.BlockSpec((1, N_HEADS, 2 * HEAD_DIM, tr), lambda b, t: (b, 0, 0, t)),
            pl.BlockSpec((1, ATT_WIDTH, tr), lambda b, t: (b, 0, t)),
            pl.BlockSpec((1, N_HEADS, 1, V_ROWS, tr), lambda b, t: (b, 0, t, 0, 0)),
            pl.BlockSpec((1, 1, 1, ATT_WIDTH), lambda b, t: (b, t, 0, 0)),
            row(CONV_WIDTH), row(D_MODEL), row(D_MODEL),
            pl.BlockSpec((1, 2, CONV_WIDTH), lambda b, t: (b, 0, 0)),
        ],
        out_shape=[
            jax.ShapeDtypeStruct((bsz, seq, ATT_WIDTH), _F32),
            jax.ShapeDtypeStruct((bsz, seq, ATT_WIDTH), _F32),
            jax.ShapeDtypeStruct((bsz, nt, N_HEADS, tr, LANES), _BF),
            jax.ShapeDtypeStruct((bsz, N_HEADS, 2 * HEAD_DIM, seq), _BF),
            jax.ShapeDtypeStruct((bsz, ATT_WIDTH, seq), _BF),
            jax.ShapeDtypeStruct((bsz, N_HEADS, nt, V_ROWS, tr), _BF),
            jax.ShapeDtypeStruct((bsz, nt, 1, ATT_WIDTH), _F32),
            jax.ShapeDtypeStruct((bsz, seq, CONV_WIDTH), _BF),
            jax.ShapeDtypeStruct((bsz, seq, D_MODEL), _F32),
            jax.ShapeDtypeStruct((bsz, seq, D_MODEL), _F32),
            jax.ShapeDtypeStruct((bsz, 2, CONV_WIDTH), _F32),
        ],
        scratch_shapes=[pltpu.VMEM((tr + 8, CONV_WIDTH), _F32)],
        compiler_params=_params(("arbitrary", "arbitrary")),
        name="inproj_prompt",
    )(x, mod, g, w_bf, cw)


def _inproj_sample_kernel(x_ref, mod_ref, g_ref, w_ref, cw_ref, prev_ref,
                          q_ref, k_ref, v_ref, cb_ref, ga_ref, gb_ref, cstate_ref, *, reps):
    rows = x_ref.shape[0]
    stride = rows // reps
    hb = _modulated_norm(x_ref[...], g_ref[...], _mod_chunk(mod_ref, 0, reps), _mod_chunk(mod_ref, 1, reps))
    q, k, v, u, bg, cg, ga, gb = _project(hb, w_ref)
    q_ref[...] = q * (HEAD_DIM ** -0.5)
    k_ref[...] = k
    v_ref[...] = v
    cu = cg * u
    ext = jnp.concatenate([prev_ref[...], cu], axis=0)
    cw = cw_ref[...]
    conv = (cw[0:1] * ext[0:rows] + cw[1:2] * ext[stride:stride + rows]
            + cw[2:3] * ext[2 * stride:2 * stride + rows])
    cb_ref[...] = (bg * conv).astype(_BF)
    cstate_ref[...] = ext[rows:rows + 2 * stride]
    ga_ref[...] = _sigmoid(ga)
    gb_ref[...] = _sigmoid(gb)


def _inproj_sample(x, mod, g, w_bf, cw, prev, reps, layer):
    rows = x.shape[0]
    full = lambda a: pl.BlockSpec(a.shape, lambda i: (0,) * a.ndim)
    shp = lambda w, dt=_F32: jax.ShapeDtypeStruct((rows, w), dt)
    outs = [shp(ATT_WIDTH), shp(ATT_WIDTH), shp(ATT_WIDTH), shp(CONV_WIDTH, _BF),
            shp(D_MODEL), shp(D_MODEL), jax.ShapeDtypeStruct(prev.shape, _F32)]
    return pl.pallas_call(
        functools.partial(_inproj_sample_kernel, reps=reps),
        grid=(1,),
        in_specs=[full(x), full(mod), full(g), _layer_spec(w_bf, layer), full(cw), full(prev)],
        out_specs=[pl.BlockSpec(o.shape, lambda i: (0, 0)) for o in outs],
        out_shape=outs,
        compiler_params=_params(("arbitrary",)),
        name="inproj_sample",
    )(x, mod, g, w_bf, cw, prev)


def _top_k_rows(gate, own):
    n = gate.shape[0]
    row = lax.broadcasted_iota(jnp.int32, gate.shape, 0).astype(_F32)
    past = row < own
    gate = jnp.where(past, gate, NEG)
    mask = jnp.where(row == own, 0.0, NEG)
    for _ in range(min(TOP_K, n)):
        mx = jnp.max(gate, axis=0, keepdims=True)
        first = jnp.min(jnp.where(gate == mx, row, float(n)), axis=0, keepdims=True)
        chosen = row == first
        mask = jnp.where(chosen, jnp.where(past, 0.0, mask), mask)
        gate = jnp.where(chosen, REMOVED, gate)
    return mask


def _attn_prompt_kernel(qtz_ref, qt_ref, kx_ref, vt_ref, kmean_ref, bias_ref, o_ref,
                        qx_ref, m_ref, acc_ref, s_ref):
    i = pl.program_id(1)
    tq = qt_ref.shape[2]
    own = i.astype(_F32)
    n_far = jnp.maximum(i - (N_BIAS_TILES - 2), 0)
    is_far = lax.broadcasted_iota(jnp.int32, (MAX_BLOCKS, tq), 0) < n_far
    for h in range(N_HEADS):
        km = kmean_ref[0, :, (h // 2) * LANES:(h // 2 + 1) * LANES].astype(_BF)
        mask = _top_k_rows(_dot(km, qtz_ref[0, h]), own)
        far_bias = bias_ref[h, N_BIAS_TILES - 1, 0:1, 0:1]
        ext = mask + jnp.where(is_far, far_bias, 0.0)
        hi = ext.astype(_BF)
        qx_ref[h, 0:HEAD_DIM] = qt_ref[0, h * HEAD_DIM:(h + 1) * HEAD_DIM, :]
        qx_ref[h, HEAD_DIM:HEAD_DIM + MAX_BLOCKS] = hi
        qx_ref[h, HEAD_DIM + MAX_BLOCKS:] = (ext - hi.astype(_F32)).astype(_BF)
    m_ref[...] = jnp.full(m_ref.shape, NEG, _F32)
    acc_ref[...] = jnp.zeros(acc_ref.shape, _F32)

    n_pairs = N_HEADS // 2
    last_block = kx_ref.shape[1] - 1

    def logits(j, pair):
        return jnp.concatenate(
            [_dot(kx_ref[0, j, 2 * pair + e], qx_ref[2 * pair + e]) for e in range(2)], axis=1)

    ahead = s_ref.shape[0]
    for pair in range(ahead):
        s_ref[pair] = logits(0, pair)

    def body(j, carry, near):
        pending = [s_ref[pair] for pair in range(ahead)]
        for pair in range(n_pairs):
            s2 = pending.pop(0)
            if pair + ahead < n_pairs:
                pending.append(logits(j, pair + ahead))
            else:
                s_ref[pair + ahead - n_pairs] = logits(jnp.minimum(j + 1, last_block), pair + ahead - n_pairs)
            for e in range(2):
                h = 2 * pair + e
                s = s2[:, e * tq:(e + 1) * tq]
                if near:
                    s = s + bias_ref[h, i - j]
                m_old = m_ref[h:h + 1, :]
                m_new = jnp.maximum(m_old, jnp.max(s, axis=0, keepdims=True))
                alpha = jnp.exp2(m_old - m_new)
                p = jnp.exp2(s - m_new)
                m_ref[h:h + 1, :] = m_new
                acc_ref[h] = alpha * acc_ref[h] + _dot(vt_ref[0, h, j], p.astype(_BF))
        return carry

    def run(lo, hi, near):
        n = hi - lo

        def four_blocks(jj, carry):
            for u in range(4):
                body(lo + 4 * jj + u, carry, near)
            return carry

        lax.fori_loop(0, n // 4, four_blocks, 0)

        @pl.when(n % 4 >= 2)
        def _():
            body(lo + (n // 4) * 4, 0, near)
            body(lo + (n // 4) * 4 + 1, 0, near)

        @pl.when(n % 2 == 1)
        def _():
            body(hi - 1, 0, near)

    run(0, n_far, False)
    run(n_far, i + 1, True)
    for h in range(N_HEADS):
        acc = acc_ref[h]
        o_ref[0, h * HEAD_DIM:(h + 1) * HEAD_DIM, :] = (acc[:HEAD_DIM] / acc[HEAD_DIM:HEAD_DIM + 1]).astype(_BF)


def _attn_prompt(qtz, qt, kx, vt, kmean, bias_p):
    bsz, _, seq = qt.shape
    nb = seq // BLOCK
    km3 = kmean.reshape(bsz, nb, ATT_WIDTH)
    per_batch = lambda shape: pl.BlockSpec(
        (1,) + shape[1:], lambda b, i: (b,) + (0,) * (len(shape) - 1), pipeline_mode=pl.Buffered(1))
    return pl.pallas_call(
        _attn_prompt_kernel,
        grid=(bsz, nb),
        in_specs=[
            pl.BlockSpec((1, N_HEADS, 2 * HEAD_DIM, BLOCK), lambda b, i: (b, 0, 0, i)),
            pl.BlockSpec((1, ATT_WIDTH, BLOCK), lambda b, i: (b, 0, i)),
            per_batch(kx.shape), per_batch(vt.shape), per_batch(km3.shape),
            _const_spec(bias_p.shape),
        ],
        out_specs=pl.BlockSpec((1, ATT_WIDTH, BLOCK), lambda b, i: (b, 0, i)),
        out_shape=jax.ShapeDtypeStruct((bsz, ATT_WIDTH, seq), _BF),
        scratch_shapes=[
            pltpu.VMEM((N_HEADS, LANES, BLOCK), _BF),
            pltpu.VMEM((N_HEADS, BLOCK), _F32),
            pltpu.VMEM((N_HEADS, V_ROWS, BLOCK), _F32),
            pltpu.VMEM((2, BLOCK, 2 * BLOCK), _F32),
        ],
        compiler_params=_params(("arbitrary", "arbitrary")),
        name="attn_prompt",
    )(qtz, qt, kx, vt, km3, bias_p)


def _attn_sample_kernel(pt_ref, q_ref, kn_ref, vn_ref, bias_ref, *refs, n_pages, n_q, group):
    del pt_ref
    o_ref = refs[2 * group * n_pages]
    for s in range(group):
        k_refs = refs[s * n_pages:(s + 1) * n_pages]
        v_refs = refs[(group + s) * n_pages:(group + s + 1) * n_pages]
        o_ref[:, s, 0, :] = _attend_sample_seq(
            q_ref[:, s, 0, :], kn_ref[:, s, 0, :], vn_ref[:, s, 0, :], bias_ref, k_refs, v_refs, n_q)


def _rows_to_page(x, n_rows):
    row = lax.broadcasted_iota(jnp.int32, (PAGE_SIZE, x.shape[1]), 0)
    out = jnp.zeros((PAGE_SIZE, x.shape[1]), _F32)
    for t in range(n_rows):
        out = jnp.where(row == t, x[t:t + 1], out)
    return out.astype(_BF)


def _attend_sample_seq(q, k_new, v_new, bias_ref, k_refs, v_refs, n_q):
    n_pages = len(k_refs)
    rows = n_q * N_HEADS
    pages_per_block = BLOCK // PAGE_SIZE
    n_blocks = n_pages // pages_per_block

    rep = jnp.concatenate([jnp.broadcast_to(q[t:t + 1], (N_HEADS, ATT_WIDTH)) for t in range(n_q)], axis=0)
    r_head = lax.broadcasted_iota(jnp.int32, (rows, ATT_WIDTH), 0) % N_HEADS
    c_head = lax.broadcasted_iota(jnp.int32, (rows, ATT_WIDTH), 1) // HEAD_DIM
    head_lanes = r_head == c_head
    q_rows = jnp.where(head_lanes, rep, 0.0).astype(_BF)

    kt_pages = [r[0, 0] for r in k_refs]
    block_sums = [
        jnp.sum(sum(kt_pages[n * pages_per_block:(n + 1) * pages_per_block]), axis=1, keepdims=True)
        for n in range(n_blocks)]
    kmean_t = jnp.concatenate(block_sums, axis=1) * (1.0 / BLOCK)
    gate = _dot(q_rows, kmean_t.astype(_BF))

    col = lax.broadcasted_iota(jnp.int32, gate.shape, 1).astype(_F32)
    sel = jnp.full(gate.shape, NEG, _F32)
    for _ in range(min(TOP_K, n_blocks)):
        mx = jnp.max(gate, axis=1, keepdims=True)
        first = jnp.min(jnp.where(gate == mx, col, float(n_blocks)), axis=1, keepdims=True)
        chosen = col == first
        sel = jnp.where(chosen, 0.0, sel)
        gate = jnp.where(chosen, REMOVED, gate)

    logits = []
    for p in range(n_pages):
        s = _dot(q_rows, kt_pages[p].astype(_BF)) + bias_ref[:, p * PAGE_SIZE:(p + 1) * PAGE_SIZE]
        n = p // pages_per_block
        logits.append(s + sel[:, n:n + 1])
    k_own = _rows_to_page(k_new, n_q)
    v_own = _rows_to_page(v_new, n_q)
    logits.append(_dot_nt(q_rows, k_own) + bias_ref[:, n_pages * PAGE_SIZE:])

    m = functools.reduce(jnp.maximum, logits)
    m = jnp.max(m, axis=1, keepdims=True)
    l = jnp.zeros((rows, 1), _F32)
    acc = jnp.zeros((rows, ATT_WIDTH), _F32)
    for p in range(n_pages + 1):
        pr = jnp.exp(logits[p] - m)
        l = l + jnp.sum(pr, axis=1, keepdims=True)
        if p == n_pages:
            acc = acc + _dot(pr.astype(_BF), v_own)
        else:
            acc = acc + _dot_nt(pr.astype(_BF), v_refs[p][0, 0].astype(_BF))
    out = jnp.where(head_lanes, acc / l, 0.0)
    return jnp.sum(out.reshape(n_q, N_HEADS, ATT_WIDTH), axis=1)


def _attn_sample(page_table, q, k_new, v_new, bias_s, cache_kt, cache_vt, layer):
    n_seq, n_pages = page_table.shape
    n_q = q.shape[0]
    group = 2
    assert n_seq % group == 0
    page_spec = lambda s, p: pl.BlockSpec(
        (1, 1, ATT_WIDTH, PAGE_SIZE), lambda g, pt: (layer, pt[group * g + s, p], 0, 0))
    pages = [page_spec(s, p) for s in range(group) for p in range(n_pages)]
    per_group = pl.BlockSpec((n_q, group, 1, ATT_WIDTH), lambda g, pt: (0, g, 0, 0))
    grid_spec = pltpu.PrefetchScalarGridSpec(
        num_scalar_prefetch=1,
        grid=(n_seq // group,),
        in_specs=[per_group, per_group, per_group,
                  pl.BlockSpec(bias_s.shape, lambda g, pt: (0, 0), pipeline_mode=pl.Buffered(1))]
                 + pages * 2,
        out_specs=per_group,
    )
    n_in = group * n_pages
    return pl.pallas_call(
        functools.partial(_attn_sample_kernel, n_pages=n_pages, n_q=n_q, group=group),
        grid_spec=grid_spec,
        out_shape=jax.ShapeDtypeStruct(q.shape, _F32),
        compiler_params=_params(("arbitrary",)),
        name="attn_sample",
    )(page_table, q, k_new, v_new, bias_s, *([cache_kt] * n_in), *([cache_vt] * n_in))


def _post_kernel(x_ref, att_ref, cb_ref, ga_ref, gb_ref, mod_ref, g_ref, wa_ref, wc_ref, wo_ref,
                 x1_ref, h2_ref, *, reps, att_transposed):
    rows = x_ref.shape[1]
    part = min(rows, BLOCK)
    parts = [slice(r, r + part) for r in range(0, rows, part)]

    def mod_rows(idx, sl):
        m = _mod_chunk(mod_ref, idx, reps)
        return m if m.shape[0] == 1 else m[sl]

    branches = []
    for sl in parts:
        if att_transposed:
            att = jnp.transpose(att_ref[0, :, sl].astype(_F32)).astype(_BF)
        else:
            att = att_ref[0, sl].astype(_BF)
        branches.append((_dot(att, wa_ref[0]), _dot(cb_ref[0, sl], wc_ref[0])))
    mixed = []
    for sl, (y_a, y_b) in zip(parts, branches):
        merged = ga_ref[0, sl] * y_a + gb_ref[0, sl] * y_b
        mixed.append(_dot(merged.astype(_BF), wo_ref[0]))
    for sl, o in zip(parts, mixed):
        x1 = x_ref[0, sl] + mod_rows(2, sl) * o
        x1_ref[0, sl] = x1
        h2 = _rmsnorm(x1, g_ref[...]) * (1.0 + mod_rows(4, sl)) + mod_rows(3, sl)
        h2_ref[0, sl] = h2.astype(_BF)


def _post(x, att, cb, ga, gb, mod, g, wa, wc, wo, *, layer, tr, att_transposed):
    bsz, seq, _ = x.shape
    nt = seq // tr
    reps = 1 if mod.shape[1] == 1 else tr // mod.shape[1]
    row = lambda w: pl.BlockSpec((1, tr, w), lambda b, t: (b, t, 0))
    att_spec = (pl.BlockSpec((1, ATT_WIDTH, tr), lambda b, t: (b, 0, t)) if att_transposed
                else row(ATT_WIDTH))
    return pl.pallas_call(
        functools.partial(_post_kernel, reps=reps, att_transposed=att_transposed),
        grid=(bsz, nt),
        in_specs=[
            row(D_MODEL), att_spec, row(CONV_WIDTH), row(D_MODEL), row(D_MODEL),
            pl.BlockSpec((1,) + mod.shape[1:], lambda b, t: (b, 0, 0)),
            _const_spec((1, D_MODEL)),
            _layer_spec(wa, layer), _layer_spec(wc, layer), _layer_spec(wo, layer),
        ],
        out_specs=[row(D_MODEL), row(D_MODEL)],
        out_shape=[jax.ShapeDtypeStruct(x.shape, _F32), jax.ShapeDtypeStruct(x.shape, _BF)],
        compiler_params=_params(("arbitrary", "arbitrary")),
        name="post_attn",
    )(x, att, cb, ga, gb, mod, g, wa, wc, wo)


def _ffn_kernel(*refs, reps, stride, final):
    if stride == 1:
        x1_ref, h2_ref, mod_ref, wu_ref, cw_ref, cbias_ref, wd_ref = refs[:7]
        rest = refs[7:]
        prev_ref = None
    else:
        x1_ref, h2_ref, mod_ref, wu_ref, cw_ref, cbias_ref, wd_ref, prev_ref = refs[:8]
        rest = refs[8:]
    if final:
        fg_ref, rest = rest[0], rest[1:]
    out_ref, state_ref = rest[0], rest[1]
    ext_ref = rest[2] if stride == 1 else None

    rows = x1_ref.shape[1]
    hb = h2_ref[0]
    cw = cw_ref[...]
    cbias = cbias_ref[...]

    if stride == 1:
        @pl.when(pl.program_id(1) == 0)
        def _():
            ext_ref[0:8, :] = jnp.zeros((8, 2 * D_FF), _F32)

    n_chunks = D_FF // FFN_CHUNK
    a_cols = lambda c: slice(c * FFN_CHUNK, (c + 1) * FFN_CHUNK)
    b_cols = lambda c: slice(D_FF + c * FFN_CHUNK, D_FF + (c + 1) * FFN_CHUNK)
    up_pair = lambda c: (_dot(hb, wu_ref[0, :, a_cols(c)]), _dot(hb, wu_ref[0, :, b_cols(c)]))

    def conv_cols(up, sl):
        if stride == 1:
            ext_ref[8:8 + rows, sl] = up
            e0, e1, e2 = ext_ref[6:6 + rows, sl], ext_ref[7:7 + rows, sl], up
        else:
            ext = jnp.concatenate([prev_ref[:, sl], up], axis=0)
            e0, e1, e2 = ext[0:rows], ext[stride:stride + rows], up
            state_ref[:, sl] = ext[rows:rows + 2 * stride]
        return cw[0:1, sl] * e0 + cw[1:2, sl] * e1 + cw[2:3, sl] * e2 + cbias[:, sl]

    acc = jnp.zeros((rows, D_MODEL), _F32)
    ahead = 4
    pending = [up_pair(c) for c in range(ahead)]
    for c in range(n_chunks):
        up_a, up_b = pending.pop(0)
        if c + ahead < n_chunks:
            pending.append(up_pair(c + ahead))
        a = conv_cols(up_a, a_cols(c))
        b = conv_cols(up_b, b_cols(c))
        acc = acc + _dot((_silu(a) * b).astype(_BF), wd_ref[0, a_cols(c), :])

    if stride == 1:
        state_ref[0] = ext_ref[rows + 6:rows + 8, :]
        ext_ref[0:8, :] = ext_ref[rows:rows + 8, :]

    x2 = x1_ref[0] + _mod_chunk(mod_ref, 5, reps) * acc
    out_ref[0] = _rmsnorm(x2, fg_ref[...]) if final else x2


def _ffn(x1, h2, mod, wu, cw, cbias, wd, prev, final_g, *, layer, tr, stride):
    bsz, seq, _ = x1.shape
    nt = seq // tr
    reps = 1 if mod.shape[1] == 1 else tr // mod.shape[1]
    final = final_g is not None
    row = lambda w: pl.BlockSpec((1, tr, w), lambda b, t: (b, t, 0))
    in_specs = [row(D_MODEL), row(D_MODEL),
                pl.BlockSpec((1,) + mod.shape[1:], lambda b, t: (b, 0, 0)),
                _layer_spec(wu, layer), _const_spec(cw.shape), _const_spec(cbias.shape), _layer_spec(wd, layer)]
    args = [x1, h2, mod, wu, cw, cbias, wd]
    if stride == 1:
        state_shape = jax.ShapeDtypeStruct((bsz, 2, 2 * D_FF), _F32)
        state_spec = pl.BlockSpec((1, 2, 2 * D_FF), lambda b, t: (b, 0, 0))
        scratch = [pltpu.VMEM((tr + 8, 2 * D_FF), _F32)]
    else:
        assert bsz == 1 and nt == 1 and tr % stride == 0
        in_specs.append(_const_spec(prev.shape))
        args.append(prev)
        state_shape = jax.ShapeDtypeStruct(prev.shape, _F32)
        state_spec = pl.BlockSpec(prev.shape, lambda b, t: (0, 0))
        scratch = []
    if final:
        in_specs.append(_const_spec(final_g.shape))
        args.append(final_g)
    return pl.pallas_call(
        functools.partial(_ffn_kernel, reps=reps, stride=stride, final=final),
        grid=(bsz, nt),
        in_specs=in_specs,
        out_specs=[row(D_MODEL), state_spec],
        out_shape=[jax.ShapeDtypeStruct(x1.shape, _F32), state_shape],
        scratch_shapes=scratch,
        compiler_params=_params(("arbitrary", "arbitrary")),
        name="conv_ffn",
    )(*args)


def kernel(x_prompt, x_sample, c_prompt, c_sample, cache_k, cache_v, state_conv, state_ffn, page_table,
           rel_bias, ada_w, ada_b, norm_mix_g, norm_ffn_g, final_norm_g, w_in, conv_w, w_att_out,
           w_conv_out, w_o, w_up, ffn_conv_w, ffn_conv_b, w_down):
    depth = w_in.shape[0]
    bsz, seq, _ = x_prompt.shape
    n_seq, n_q, _ = x_sample.shape
    n_pages = page_table.shape[1]
    past_len = n_pages * PAGE_SIZE
    assert seq % BLOCK == 0 and past_len % BLOCK == 0 and n_q <= 8 and bsz <= 8
    assert cache_k.shape[2] == PAGE_SIZE

    pad_rows = 8
    c_all = jnp.concatenate([c_prompt, jnp.zeros((pad_rows - bsz, D_MODEL), _F32), c_sample], axis=0)
    mod_all = _modulation(c_all, ada_w, ada_b)
    mod_p = mod_all[:, :bsz].reshape(depth, bsz, 1, -1)
    mod_s = mod_all[:, pad_rows:].reshape(depth, 1, n_seq, -1)

    bias_p, bias_s8 = _bias_tables(rel_bias, past_len)
    bias_s = bias_s8[:, :n_q].transpose(1, 0, 2).reshape(n_q * N_HEADS, -1)

    page_t = lambda c: c.transpose(0, 1, 3, 4, 2).reshape(c.shape[0], c.shape[1], ATT_WIDTH, PAGE_SIZE)
    cache_kt, cache_vt = page_t(cache_k), page_t(cache_v)

    time_major = lambda a: a.transpose(1, 0, 2).reshape(-1, a.shape[2])
    seq_major = lambda a, steps: a.reshape(steps, n_seq, -1).transpose(1, 0, 2)
    rows_s = n_q * n_seq

    w_in_bf = w_in.astype(_BF)
    wa, wc, wo = w_att_out.astype(_BF), w_conv_out.astype(_BF), w_o.astype(_BF)
    wu, wd = w_up.astype(_BF), w_down.astype(_BF)
    conv_prev = state_conv.transpose(0, 2, 1, 3).reshape(depth, 2 * n_seq, -1)
    ffn_prev = state_ffn.transpose(0, 2, 1, 3).reshape(depth, 2 * n_seq, -1)
    by_seq = lambda a: a.reshape(n_q, n_seq, 1, -1)

    xp = x_prompt
    xs = time_major(x_sample)[None]
    fin = final_norm_g.reshape(1, D_MODEL)
    outs = [[] for _ in range(8)]
    for l in range(depth):
        g_mix = norm_mix_g[l].reshape(1, D_MODEL)
        g_ffn = norm_ffn_g[l].reshape(1, D_MODEL)
        cbias = ffn_conv_b[l].reshape(1, -1)
        final_g = fin if l == depth - 1 else None

        k, v, kx, qtz, qt, vt, kmean, cb, ga, gb, cstate = _inproj_prompt(
            xp, mod_p[l], g_mix, w_in_bf, conv_w[l], l)
        att_t = _attn_prompt(qtz, qt, kx, vt, kmean, bias_p)
        x1, h2 = _post(xp, att_t, cb, ga, gb, mod_p[l], g_ffn, wa, wc, wo,
                       layer=l, tr=2 * BLOCK, att_transposed=True)
        xp, fstate = _ffn(x1, h2, mod_p[l], wu, ffn_conv_w[l], cbias, wd, None, final_g,
                          layer=l, tr=BLOCK, stride=1)
        for dst, val in zip(outs[:4], (k, v, cstate, fstate)):
            dst.append(val)

        q_s, k_s, v_s, cb_s, ga_s, gb_s, cstate_s = _inproj_sample(
            xs[0], mod_s[l], g_mix, w_in_bf, conv_w[l], conv_prev[l], n_q, l)
        att_s = _attn_sample(page_table, by_seq(q_s), by_seq(k_s), by_seq(v_s), bias_s, cache_kt, cache_vt, l)
        x1s, h2s = _post(xs, att_s.reshape(1, rows_s, ATT_WIDTH), cb_s[None], ga_s[None], gb_s[None],
                         mod_s[l], g_ffn, wa, wc, wo, layer=l, tr=rows_s, att_transposed=False)
        xs, fstate_s = _ffn(x1s, h2s, mod_s[l], wu, ffn_conv_w[l], cbias, wd, ffn_prev[l], final_g,
                            layer=l, tr=rows_s, stride=n_seq)
        for dst, val in zip(outs[4:], (k_s, v_s, cstate_s, fstate_s)):
            dst.append(val)

    kp, vp, cp, fp, ks, vs, cs, fs = (jnp.stack(o) for o in outs)
    heads = lambda a: a.reshape(a.shape[:-1] + (N_HEADS, HEAD_DIM))
    to_seq_major = lambda a, steps: a.reshape(depth, steps, n_seq, -1).transpose(0, 2, 1, 3)
    return (xp, seq_major(xs[0], n_q), heads(kp), heads(vp), cp, fp,
            heads(to_seq_major(ks, n_q)), heads(to_seq_major(vs, n_q)), to_seq_major(cs, 2), to_seq_major(fs, 2))
```

```python
import functools

import jax
import jax.numpy as jnp
from jax import lax
from jax.experimental import pallas as pl
from jax.experimental.pallas import tpu as pltpu

D_MODEL = 1024
N_HEADS = 8
HEAD_DIM = 64
ATT_WIDTH = N_HEADS * HEAD_DIM
CONV_WIDTH = 512
BLOCK = 256
TOP_K = 3
N_BUCKETS = 32
MAX_DIST = 1024
D_FF = 2816
PAGE_SIZE = 128
EPS = 1e-6
NEG = -1e30
REMOVED = -3e38
LOG2E = 1.4426950408889634
N_IN = 3 * ATT_WIDTH + 3 * CONV_WIDTH + 2 * D_MODEL
LANES = 128
FFN_CHUNK = 256
N_BIAS_TILES = 6
MAX_BLOCKS = 32
V_ROWS = HEAD_DIM + 16
VMEM_LIMIT = 56 * 1024 * 1024

_BF = jnp.bfloat16
_F32 = jnp.float32


def _bucket_upper_bounds():
    max_exact = N_BUCKETS // 2
    n_log = N_BUCKETS - max_exact
    ratio = MAX_DIST // max_exact
    assert ratio * max_exact == MAX_DIST
    ups = [b + 1 for b in range(max_exact)]
    for k in range(1, n_log):
        target = (max_exact ** n_log) * (ratio ** k)
        n = max_exact
        while n ** n_log < target:
            n += 1
        ups.append(n)
    return ups


_BUCKET_UPPER = _bucket_upper_bounds()
assert (N_BIAS_TILES - 2) * BLOCK + 1 >= _BUCKET_UPPER[-1]


def _dot(a, b):
    return jnp.dot(a, b, preferred_element_type=_F32)


def _dot_nt(a, b):
    return lax.dot_general(a, b, (((1,), (1,)), ((), ())), preferred_element_type=_F32)


def _sigmoid(x):
    return 1.0 / (1.0 + jnp.exp(-x))


def _silu(x):
    return x * _sigmoid(x)


def _rmsnorm(x, g):
    return x * lax.rsqrt(jnp.mean(x * x, axis=-1, keepdims=True) + EPS) * g


def _mod_chunk(mod_ref, idx, reps):
    v = mod_ref[0, :, idx * D_MODEL:(idx + 1) * D_MODEL]
    if reps > 1:
        v = jnp.concatenate([v] * reps, axis=0)
    return v


def _params(sem, vmem=VMEM_LIMIT):
    return pltpu.CompilerParams(dimension_semantics=sem, vmem_limit_bytes=vmem)


def _const_spec(shape):
    nd = len(shape)
    return pl.BlockSpec(shape, lambda *_: (0,) * nd, pipeline_mode=pl.Buffered(1))


def _layer_spec(stacked, layer):
    return pl.BlockSpec((1,) + stacked.shape[1:], lambda *_: (layer, 0, 0), pipeline_mode=pl.Buffered(1))


def _mod_kernel(c_ref, w_ref, b_ref, o_ref):
    a = _silu(c_ref[...]).astype(_BF)
    o_ref[0] = _dot(a, w_ref[0].astype(_BF)) + b_ref[0]


def _modulation(c_all, ada_w, ada_b):
    depth = ada_w.shape[0]
    rows = c_all.shape[0]
    n_chunks = ada_w.shape[2] // D_MODEL
    return pl.pallas_call(
        _mod_kernel,
        grid=(depth, n_chunks),
        in_specs=[
            pl.BlockSpec((rows, D_MODEL), lambda l, n: (0, 0)),
            pl.BlockSpec((1, D_MODEL, D_MODEL), lambda l, n: (l, 0, n)),
            pl.BlockSpec((1, 1, D_MODEL), lambda l, n: (l, 0, n)),
        ],
        out_specs=pl.BlockSpec((1, rows, D_MODEL), lambda l, n: (l, 0, n)),
        out_shape=jax.ShapeDtypeStruct((depth, rows, ada_w.shape[2]), _F32),
        compiler_params=_params(("arbitrary", "arbitrary")),
        name="modulation",
    )(c_all, ada_w, ada_b.reshape(depth, 1, -1))


def _bias_kernel(rb_ref, bp_ref, bs_ref, *, past_len):
    h = pl.program_id(0)

    def table(dist, scale):
        n = jnp.maximum(dist, 0)
        v = jnp.full(dist.shape, rb_ref[N_BUCKETS - 1, h], _F32)
        for b in range(N_BUCKETS - 2, -1, -1):
            v = jnp.where(n < _BUCKET_UPPER[b], rb_ref[b, h], v)
        return jnp.where(dist >= 0, v * scale, NEG)

    kk = lax.broadcasted_iota(jnp.int32, (BLOCK, BLOCK), 0)
    qq = lax.broadcasted_iota(jnp.int32, (BLOCK, BLOCK), 1)
    for d in range(N_BIAS_TILES):
        bp_ref[0, d] = table(d * BLOCK + qq - kk, LOG2E)
    t = lax.broadcasted_iota(jnp.int32, bs_ref.shape[1:], 0)
    p = lax.broadcasted_iota(jnp.int32, bs_ref.shape[1:], 1)
    bs_ref[0] = table(past_len + t - p, 1.0)


def _bias_tables(rel_bias, past_len):
    s_cols = past_len + PAGE_SIZE
    return pl.pallas_call(
        functools.partial(_bias_kernel, past_len=past_len),
        grid=(N_HEADS,),
        in_specs=[pl.BlockSpec(memory_space=pltpu.SMEM)],
        out_specs=[
            pl.BlockSpec((1, N_BIAS_TILES, BLOCK, BLOCK), lambda h: (h, 0, 0, 0)),
            pl.BlockSpec((1, 8, s_cols), lambda h: (h, 0, 0)),
        ],
        out_shape=[
            jax.ShapeDtypeStruct((N_HEADS, N_BIAS_TILES, BLOCK, BLOCK), _F32),
            jax.ShapeDtypeStruct((N_HEADS, 8, s_cols), _F32),
        ],
        compiler_params=_params(("arbitrary",)),
        name="bias_tables",
    )(rel_bias)


def _modulated_norm(x, g, sh, sc):
    return (_rmsnorm(x, g) * (1.0 + sc) + sh).astype(_BF)


_GROUP_BOUNDS = (0, ATT_WIDTH, 2 * ATT_WIDTH, 3 * ATT_WIDTH, 3 * ATT_WIDTH + CONV_WIDTH,
                 3 * ATT_WIDTH + 2 * CONV_WIDTH, 3 * ATT_WIDTH + 3 * CONV_WIDTH,
                 3 * ATT_WIDTH + 3 * CONV_WIDTH + D_MODEL, N_IN)


def _project(hb, w_ref):
    return [_dot(hb, w_ref[0, :, a:b]) for a, b in zip(_GROUP_BOUNDS[:-1], _GROUP_BOUNDS[1:])]


def _inproj_prompt_kernel(x_ref, mod_ref, g_ref, w_ref, cw_ref,
                          k_ref, v_ref, kx_ref, qtz_ref, qt_ref, vt_ref, kmean_ref, cb_ref, ga_ref, gb_ref,
                          cstate_ref, ext_ref):
    tr = x_ref.shape[1]
    t = pl.program_id(1)

    @pl.when(t == 0)
    def _():
        ext_ref[0:8, :] = jnp.zeros((8, CONV_WIDTH), _F32)

    hb = _modulated_norm(x_ref[0], g_ref[...], _mod_chunk(mod_ref, 0, 1), _mod_chunk(mod_ref, 1, 1))
    q, k, v, u, bg, cg, ga, gb = _project(hb, w_ref)

    qt = jnp.transpose(q * (HEAD_DIM ** -0.5))
    zeros = jnp.zeros((HEAD_DIM, tr), _F32)
    for h in range(N_HEADS):
        qh = qt[h * HEAD_DIM:(h + 1) * HEAD_DIM]
        pair = [qh, zeros] if h % 2 == 0 else [zeros, qh]
        qtz_ref[0, h] = jnp.concatenate(pair, axis=0).astype(_BF)
    qt_ref[0] = (qt * LOG2E).astype(_BF)

    k_ref[0] = k
    kmean_ref[0, 0] = jnp.mean(k, axis=0, keepdims=True)
    lane = lax.broadcasted_iota(jnp.int32, (tr, LANES), 1)
    upper = jnp.where((lane == HEAD_DIM + t) | (lane == HEAD_DIM + MAX_BLOCKS + t), 1.0, 0.0)
    for pair in range(N_HEADS // 2):
        slab = k[:, pair * LANES:(pair + 1) * LANES]
        for e, k_low in enumerate((slab, pltpu.roll(slab, HEAD_DIM, axis=1))):
            kx_ref[0, 0, 2 * pair + e] = jnp.where(lane < HEAD_DIM, k_low, upper).astype(_BF)

    v_ref[0] = v
    vt = jnp.transpose(v)
    ones = jnp.ones((V_ROWS - HEAD_DIM, tr), _BF)
    for h in range(N_HEADS):
        vt_ref[0, h, 0] = jnp.concatenate([vt[h * HEAD_DIM:(h + 1) * HEAD_DIM].astype(_BF), ones], axis=0)

    ext_ref[8:8 + tr, :] = cg * u
    cw = cw_ref[...]
    conv = (cw[0:1] * ext_ref[6:6 + tr, :] + cw[1:2] * ext_ref[7:7 + tr, :]
            + cw[2:3] * ext_ref[8:8 + tr, :])
    cb_ref[0] = (bg * conv).astype(_BF)
    cstate_ref[0] = ext_ref[tr + 6:tr + 8, :]
    ext_ref[0:8, :] = ext_ref[tr:tr + 8, :]

    ga_ref[0] = _sigmoid(ga)
    gb_ref[0] = _sigmoid(gb)


def _inproj_prompt(x, mod, g, w_bf, cw, layer):
    bsz, seq, _ = x.shape
    tr = BLOCK
    nt = seq // tr
    assert nt == MAX_BLOCKS
    row = lambda w: pl.BlockSpec((1, tr, w), lambda b, t: (b, t, 0))
    return pl.pallas_call(
        _inproj_prompt_kernel,
        grid=(bsz, nt),
        in_specs=[
            row(D_MODEL),
            pl.BlockSpec((1, 1, mod.shape[2]), lambda b, t: (b, 0, 0)),
            _const_spec((1, D_MODEL)),
            _layer_spec(w_bf, layer),
            _const_spec((3, CONV_WIDTH)),
        ],
        out_specs=[
            row(ATT_WIDTH), row(ATT_WIDTH),
            pl.BlockSpec((1, 1, N_HEADS, tr, LANES), lambda b, t: (b, t, 0, 0, 0)),
            pl.BlockSpec((1, N_HEADS, 2 * HEAD_DIM, tr), lambda b, t: (b, 0, 0, t)),
            pl.BlockSpec((1, ATT_WIDTH, tr), lambda b, t: (b, 0, t)),
            pl.BlockSpec((1, N_HEADS, 1, V_ROWS, tr), lambda b, t: (b, 0, t, 0, 0)),
            pl.BlockSpec((1, 1, 1, ATT_WIDTH), lambda b, t: (b, t, 0, 0)),
            row(CONV_WIDTH), row(D_MODEL), row(D_MODEL),
            pl.BlockSpec((1, 2, CONV_WIDTH), lambda b, t: (b, 0, 0)),
        ],
        out_shape=[
            jax.ShapeDtypeStruct((bsz, seq, ATT_WIDTH), _F32),
            jax.ShapeDtypeStruct((bsz, seq, ATT_WIDTH), _F32),
            jax.ShapeDtypeStruct((bsz, nt, N_HEADS, tr, LANES), _BF),
            jax.ShapeDtypeStruct((bsz, N_HEADS, 2 * HEAD_DIM, seq), _BF),
            jax.ShapeDtypeStruct((bsz, ATT_WIDTH, seq), _BF),
            jax.ShapeDtypeStruct((bsz, N_HEADS, nt, V_ROWS, tr), _BF),
            jax.ShapeDtypeStruct((bsz, nt, 1, ATT_WIDTH), _F32),
            jax.ShapeDtypeStruct((bsz, seq, CONV_WIDTH), _BF),
            jax.ShapeDtypeStruct((bsz, seq, D_MODEL), _F32),
            jax.ShapeDtypeStruct((bsz, seq, D_MODEL), _F32),
            jax.ShapeDtypeStruct((bsz, 2, CONV_WIDTH), _F32),
        ],
        scratch_shapes=[pltpu.VMEM((tr + 8, CONV_WIDTH), _F32)],
        compiler_params=_params(("arbitrary", "arbitrary")),
        name="inproj_prompt",
    )(x, mod, g, w_bf, cw)


def _inproj_sample_kernel(x_ref, mod_ref, g_ref, w_ref, cw_ref, prev_ref,
                          q_ref, k_ref, v_ref, cb_ref, ga_ref, gb_ref, cstate_ref, *, reps):
    rows = x_ref.shape[0]
    stride = rows // reps
    hb = _modulated_norm(x_ref[...], g_ref[...], _mod_chunk(mod_ref, 0, reps), _mod_chunk(mod_ref, 1, reps))
    q, k, v, u, bg, cg, ga, gb = _project(hb, w_ref)
    q_ref[...] = q * (HEAD_DIM ** -0.5)
    k_ref[...] = k
    v_ref[...] = v
    cu = cg * u
    ext = jnp.concatenate([prev_ref[...], cu], axis=0)
    cw = cw_ref[...]
    conv = (cw[0:1] * ext[0:rows] + cw[1:2] * ext[stride:stride + rows]
            + cw[2:3] * ext[2 * stride:2 * stride + rows])
    cb_ref[...] = (bg * conv).astype(_BF)
    cstate_ref[...] = ext[rows:rows + 2 * stride]
    ga_ref[...] = _sigmoid(ga)
    gb_ref[...] = _sigmoid(gb)


def _inproj_sample(x, mod, g, w_bf, cw, prev, reps, layer):
    rows = x.shape[0]
    full = lambda a: pl.BlockSpec(a.shape, lambda i: (0,) * a.ndim)
    shp = lambda w, dt=_F32: jax.ShapeDtypeStruct((rows, w), dt)
    outs = [shp(ATT_WIDTH), shp(ATT_WIDTH), shp(ATT_WIDTH), shp(CONV_WIDTH, _BF),
            shp(D_MODEL), shp(D_MODEL), jax.ShapeDtypeStruct(prev.shape, _F32)]
    return pl.pallas_call(
        functools.partial(_inproj_sample_kernel, reps=reps),
        grid=(1,),
        in_specs=[full(x), full(mod), full(g), _layer_spec(w_bf, layer), full(cw), full(prev)],
        out_specs=[pl.BlockSpec(o.shape, lambda i: (0, 0)) for o in outs],
        out_shape=outs,
        compiler_params=_params(("arbitrary",)),
        name="inproj_sample",
    )(x, mod, g, w_bf, cw, prev)


def _top_k_rows(gate, own):
    n = gate.shape[0]
    row = lax.broadcasted_iota(jnp.int32, gate.shape, 0).astype(_F32)
    past = row < own
    gate = jnp.where(past, gate, NEG)
    mask = jnp.where(row == own, 0.0, NEG)
    for _ in range(min(TOP_K, n)):
        mx = jnp.max(gate, axis=0, keepdims=True)
        first = jnp.min(jnp.where(gate == mx, row, float(n)), axis=0, keepdims=True)
        chosen = row == first
        mask = jnp.where(chosen, jnp.where(past, 0.0, mask), mask)
        gate = jnp.where(chosen, REMOVED, gate)
    return mask


def _attn_prompt_kernel(qtz_ref, qt_ref, kx_ref, vt_ref, kmean_ref, bias_ref, o_ref,
                        qx_ref, m_ref, acc_ref, s_ref):
    i = pl.program_id(1)
    tq = qt_ref.shape[2]
    own = i.astype(_F32)
    n_far = jnp.maximum(i - (N_BIAS_TILES - 2), 0)
    is_far = lax.broadcasted_iota(jnp.int32, (MAX_BLOCKS, tq), 0) < n_far
    for h in range(N_HEADS):
        km = kmean_ref[0, :, (h // 2) * LANES:(h // 2 + 1) * LANES].astype(_BF)
        mask = _top_k_rows(_dot(km, qtz_ref[0, h]), own)
        far_bias = bias_ref[h, N_BIAS_TILES - 1, 0:1, 0:1]
        ext = mask + jnp.where(is_far, far_bias, 0.0)
        hi = ext.astype(_BF)
        qx_ref[h, 0:HEAD_DIM] = qt_ref[0, h * HEAD_DIM:(h + 1) * HEAD_DIM, :]
        qx_ref[h, HEAD_DIM:HEAD_DIM + MAX_BLOCKS] = hi
        qx_ref[h, HEAD_DIM + MAX_BLOCKS:] = (ext - hi.astype(_F32)).astype(_BF)
    m_ref[...] = jnp.full(m_ref.shape, NEG, _F32)
    acc_ref[...] = jnp.zeros(acc_ref.shape, _F32)

    n_pairs = N_HEADS // 2
    last_block = kx_ref.shape[1] - 1

    def logits(j, pair):
        return jnp.concatenate(
            [_dot(kx_ref[0, j, 2 * pair + e], qx_ref[2 * pair + e]) for e in range(2)], axis=1)

    ahead = s_ref.shape[0]
    for pair in range(ahead):
        s_ref[pair] = logits(0, pair)

    def body(j, carry, near):
        pending = [s_ref[pair] for pair in range(ahead)]
        for pair in range(n_pairs):
            s2 = pending.pop(0)
            if pair + ahead < n_pairs:
                pending.append(logits(j, pair + ahead))
            else:
                s_ref[pair + ahead - n_pairs] = logits(jnp.minimum(j + 1, last_block), pair + ahead - n_pairs)
            for e in range(2):
                h = 2 * pair + e
                s = s2[:, e * tq:(e + 1) * tq]
                if near:
                    s = s + bias_ref[h, i - j]
                m_old = m_ref[h:h + 1, :]
                m_new = jnp.maximum(m_old, jnp.max(s, axis=0, keepdims=True))
                alpha = jnp.exp2(m_old - m_new)
                p = jnp.exp2(s - m_new)
                m_ref[h:h + 1, :] = m_new
                acc_ref[h] = alpha * acc_ref[h] + _dot(vt_ref[0, h, j], p.astype(_BF))
        return carry

    def run(lo, hi, near):
        n = hi - lo

        def four_blocks(jj, carry):
            for u in range(4):
                body(lo + 4 * jj + u, carry, near)
            return carry

        lax.fori_loop(0, n // 4, four_blocks, 0)

        @pl.when(n % 4 >= 2)
        def _():
            body(lo + (n // 4) * 4, 0, near)
            body(lo + (n // 4) * 4 + 1, 0, near)

        @pl.when(n % 2 == 1)
        def _():
            body(hi - 1, 0, near)

    run(0, n_far, False)
    run(n_far, i + 1, True)
    for h in range(N_HEADS):
        acc = acc_ref[h]
        o_ref[0, h * HEAD_DIM:(h + 1) * HEAD_DIM, :] = (acc[:HEAD_DIM] / acc[HEAD_DIM:HEAD_DIM + 1]).astype(_BF)


def _attn_prompt(qtz, qt, kx, vt, kmean, bias_p):
    bsz, _, seq = qt.shape
    nb = seq // BLOCK
    km3 = kmean.reshape(bsz, nb, ATT_WIDTH)
    per_batch = lambda shape: pl.BlockSpec(
        (1,) + shape[1:], lambda b, i: (b,) + (0,) * (len(shape) - 1), pipeline_mode=pl.Buffered(1))
    return pl.pallas_call(
        _attn_prompt_kernel,
        grid=(bsz, nb),
        in_specs=[
            pl.BlockSpec((1, N_HEADS, 2 * HEAD_DIM, BLOCK), lambda b, i: (b, 0, 0, i)),
            pl.BlockSpec((1, ATT_WIDTH, BLOCK), lambda b, i: (b, 0, i)),
            per_batch(kx.shape), per_batch(vt.shape), per_batch(km3.shape),
            _const_spec(bias_p.shape),
        ],
        out_specs=pl.BlockSpec((1, ATT_WIDTH, BLOCK), lambda b, i: (b, 0, i)),
        out_shape=jax.ShapeDtypeStruct((bsz, ATT_WIDTH, seq), _BF),
        scratch_shapes=[
            pltpu.VMEM((N_HEADS, LANES, BLOCK), _BF),
            pltpu.VMEM((N_HEADS, BLOCK), _F32),
            pltpu.VMEM((N_HEADS, V_ROWS, BLOCK), _F32),
            pltpu.VMEM((2, BLOCK, 2 * BLOCK), _F32),
        ],
        compiler_params=_params(("arbitrary", "arbitrary")),
        name="attn_prompt",
    )(qtz, qt, kx, vt, km3, bias_p)


def _attn_sample_kernel(pt_ref, q_ref, kn_ref, vn_ref, bias_ref, *refs, n_pages, n_q, group):
    del pt_ref
    o_ref = refs[2 * group * n_pages]
    stages = [
        _attend_sample_seq(q_ref[:, s, 0, :], kn_ref[:, s, 0, :], vn_ref[:, s, 0, :], bias_ref,
                           refs[s * n_pages:(s + 1) * n_pages],
                           refs[(group + s) * n_pages:(group + s + 1) * n_pages], n_q)
        for s in range(group)]
    for _ in range(2):
        for seq in stages:
            next(seq)
    for s, seq in enumerate(stages):
        o_ref[:, s, 0, :] = next(seq)


def _rows_to_page(x, n_rows):
    row = lax.broadcasted_iota(jnp.int32, (PAGE_SIZE, x.shape[1]), 0)
    out = jnp.zeros((PAGE_SIZE, x.shape[1]), _F32)
    for t in range(n_rows):
        out = jnp.where(row == t, x[t:t + 1], out)
    return out.astype(_BF)


def _attend_sample_seq(q, k_new, v_new, bias_ref, k_refs, v_refs, n_q):
    n_pages = len(k_refs)
    rows = n_q * N_HEADS
    pages_per_block = BLOCK // PAGE_SIZE
    n_blocks = n_pages // pages_per_block

    rep = jnp.concatenate([jnp.broadcast_to(q[t:t + 1], (N_HEADS, ATT_WIDTH)) for t in range(n_q)], axis=0)
    r_head = lax.broadcasted_iota(jnp.int32, (rows, ATT_WIDTH), 0) % N_HEADS
    c_head = lax.broadcasted_iota(jnp.int32, (rows, ATT_WIDTH), 1) // HEAD_DIM
    head_lanes = r_head == c_head
    q_rows = jnp.where(head_lanes, rep, 0.0).astype(_BF)

    kt_pages = [r[0, 0] for r in k_refs]
    block_sums = [
        jnp.sum(sum(kt_pages[n * pages_per_block:(n + 1) * pages_per_block]), axis=1, keepdims=True)
        for n in range(n_blocks)]
    kmean_t = jnp.concatenate(block_sums, axis=1) * (1.0 / BLOCK)
    gate = _dot(q_rows, kmean_t.astype(_BF))

    col = lax.broadcasted_iota(jnp.int32, gate.shape, 1).astype(_F32)
    sel = jnp.full(gate.shape, NEG, _F32)
    for _ in range(min(TOP_K, n_blocks)):
        mx = jnp.max(gate, axis=1, keepdims=True)
        first = jnp.min(jnp.where(gate == mx, col, float(n_blocks)), axis=1, keepdims=True)
        chosen = col == first
        sel = jnp.where(chosen, 0.0, sel)
        gate = jnp.where(chosen, REMOVED, gate)
    yield

    logits = []
    for p in range(n_pages):
        s = _dot(q_rows, kt_pages[p].astype(_BF)) + bias_ref[:, p * PAGE_SIZE:(p + 1) * PAGE_SIZE]
        n = p // pages_per_block
        logits.append(s + sel[:, n:n + 1])
    k_own = _rows_to_page(k_new, n_q)
    v_own = _rows_to_page(v_new, n_q)
    logits.append(_dot_nt(q_rows, k_own) + bias_ref[:, n_pages * PAGE_SIZE:])
    yield

    m = functools.reduce(jnp.maximum, logits)
    m = jnp.max(m, axis=1, keepdims=True)
    l = jnp.zeros((rows, 1), _F32)
    acc = jnp.zeros((rows, ATT_WIDTH), _F32)
    for p in range(n_pages + 1):
        pr = jnp.exp(logits[p] - m)
        l = l + jnp.sum(pr, axis=1, keepdims=True)
        if p == n_pages:
            acc = acc + _dot(pr.astype(_BF), v_own)
        else:
            acc = acc + _dot_nt(pr.astype(_BF), v_refs[p][0, 0].astype(_BF))
    out = jnp.where(head_lanes, acc / l, 0.0)
    yield jnp.sum(out.reshape(n_q, N_HEADS, ATT_WIDTH), axis=1)


def _attn_sample(page_table, q, k_new, v_new, bias_s, cache_kt, cache_vt, layer):
    n_seq, n_pages = page_table.shape
    n_q = q.shape[0]
    group = 2
    assert n_seq % group == 0
    page_spec = lambda s, p: pl.BlockSpec(
        (1, 1, ATT_WIDTH, PAGE_SIZE), lambda g, pt: (layer, pt[group * g + s, p], 0, 0))
    pages = [page_spec(s, p) for s in range(group) for p in range(n_pages)]
    per_group = pl.BlockSpec((n_q, group, 1, ATT_WIDTH), lambda g, pt: (0, g, 0, 0))
    grid_spec = pltpu.PrefetchScalarGridSpec(
        num_scalar_prefetch=1,
        grid=(n_seq // group,),
        in_specs=[per_group, per_group, per_group,
                  pl.BlockSpec(bias_s.shape, lambda g, pt: (0, 0), pipeline_mode=pl.Buffered(1))]
                 + pages * 2,
        out_specs=per_group,
    )
    n_in = group * n_pages
    return pl.pallas_call(
        functools.partial(_attn_sample_kernel, n_pages=n_pages, n_q=n_q, group=group),
        grid_spec=grid_spec,
        out_shape=jax.ShapeDtypeStruct(q.shape, _F32),
        compiler_params=_params(("arbitrary",)),
        name="attn_sample",
    )(page_table, q, k_new, v_new, bias_s, *([cache_kt] * n_in), *([cache_vt] * n_in))


def _post_kernel(x_ref, att_ref, cb_ref, ga_ref, gb_ref, mod_ref, g_ref, wa_ref, wc_ref, wo_ref,
                 x1_ref, h2_ref, *, reps, att_transposed):
    rows = x_ref.shape[1]
    part = min(rows, BLOCK)
    parts = [slice(r, r + part) for r in range(0, rows, part)]

    def mod_rows(idx, sl):
        m = _mod_chunk(mod_ref, idx, reps)
        return m if m.shape[0] == 1 else m[sl]

    branches = []
    for sl in parts:
        if att_transposed:
            att = jnp.transpose(att_ref[0, :, sl].astype(_F32)).astype(_BF)
        else:
            att = att_ref[0, sl].astype(_BF)
        branches.append((_dot(att, wa_ref[0]), _dot(cb_ref[0, sl], wc_ref[0])))
    mixed = []
    for sl, (y_a, y_b) in zip(parts, branches):
        merged = ga_ref[0, sl] * y_a + gb_ref[0, sl] * y_b
        mixed.append(_dot(merged.astype(_BF), wo_ref[0]))
    for sl, o in zip(parts, mixed):
        x1 = x_ref[0, sl] + mod_rows(2, sl) * o
        x1_ref[0, sl] = x1
        h2 = _rmsnorm(x1, g_ref[...]) * (1.0 + mod_rows(4, sl)) + mod_rows(3, sl)
        h2_ref[0, sl] = h2.astype(_BF)


def _post(x, att, cb, ga, gb, mod, g, wa, wc, wo, *, layer, tr, att_transposed):
    bsz, seq, _ = x.shape
    nt = seq // tr
    reps = 1 if mod.shape[1] == 1 else tr // mod.shape[1]
    row = lambda w: pl.BlockSpec((1, tr, w), lambda b, t: (b, t, 0))
    att_spec = (pl.BlockSpec((1, ATT_WIDTH, tr), lambda b, t: (b, 0, t)) if att_transposed
                else row(ATT_WIDTH))
    return pl.pallas_call(
        functools.partial(_post_kernel, reps=reps, att_transposed=att_transposed),
        grid=(bsz, nt),
        in_specs=[
            row(D_MODEL), att_spec, row(CONV_WIDTH), row(D_MODEL), row(D_MODEL),
            pl.BlockSpec((1,) + mod.shape[1:], lambda b, t: (b, 0, 0)),
            _const_spec((1, D_MODEL)),
            _layer_spec(wa, layer), _layer_spec(wc, layer), _layer_spec(wo, layer),
        ],
        out_specs=[row(D_MODEL), row(D_MODEL)],
        out_shape=[jax.ShapeDtypeStruct(x.shape, _F32), jax.ShapeDtypeStruct(x.shape, _BF)],
        compiler_params=_params(("arbitrary", "arbitrary")),
        name="post_attn",
    )(x, att, cb, ga, gb, mod, g, wa, wc, wo)


def _ffn_kernel(*refs, reps, stride, final):
    if stride == 1:
        x1_ref, h2_ref, mod_ref, wu_ref, cw_ref, cbias_ref, wd_ref = refs[:7]
        rest = refs[7:]
        prev_ref = None
    else:
        x1_ref, h2_ref, mod_ref, wu_ref, cw_ref, cbias_ref, wd_ref, prev_ref = refs[:8]
        rest = refs[8:]
    if final:
        fg_ref, rest = rest[0], rest[1:]
    out_ref, state_ref = rest[0], rest[1]
    ext_ref = rest[2] if stride == 1 else None

    rows = x1_ref.shape[1]
    hb = h2_ref[0]
    cw = cw_ref[...]
    cbias = cbias_ref[...]

    if stride == 1:
        @pl.when(pl.program_id(1) == 0)
        def _():
            ext_ref[0:8, :] = jnp.zeros((8, 2 * D_FF), _F32)

    n_chunks = D_FF // FFN_CHUNK
    a_cols = lambda c: slice(c * FFN_CHUNK, (c + 1) * FFN_CHUNK)
    b_cols = lambda c: slice(D_FF + c * FFN_CHUNK, D_FF + (c + 1) * FFN_CHUNK)
    up_pair = lambda c: (_dot(hb, wu_ref[0, :, a_cols(c)]), _dot(hb, wu_ref[0, :, b_cols(c)]))

    def conv_cols(up, sl):
        if stride == 1:
            ext_ref[8:8 + rows, sl] = up
            e0, e1, e2 = ext_ref[6:6 + rows, sl], ext_ref[7:7 + rows, sl], up
        else:
            ext = jnp.concatenate([prev_ref[:, sl], up], axis=0)
            e0, e1, e2 = ext[0:rows], ext[stride:stride + rows], up
            state_ref[:, sl] = ext[rows:rows + 2 * stride]
        return cw[0:1, sl] * e0 + cw[1:2, sl] * e1 + cw[2:3, sl] * e2 + cbias[:, sl]

    acc = jnp.zeros((rows, D_MODEL), _F32)
    ahead = 4
    pending = [up_pair(c) for c in range(ahead)]
    for c in range(n_chunks):
        up_a, up_b = pending.pop(0)
        if c + ahead < n_chunks:
            pending.append(up_pair(c + ahead))
        a = conv_cols(up_a, a_cols(c))
        b = conv_cols(up_b, b_cols(c))
        acc = acc + _dot((_silu(a) * b).astype(_BF), wd_ref[0, a_cols(c), :])

    if stride == 1:
        state_ref[0] = ext_ref[rows + 6:rows + 8, :]
        ext_ref[0:8, :] = ext_ref[rows:rows + 8, :]

    x2 = x1_ref[0] + _mod_chunk(mod_ref, 5, reps) * acc
    out_ref[0] = _rmsnorm(x2, fg_ref[...]) if final else x2


def _ffn(x1, h2, mod, wu, cw, cbias, wd, prev, final_g, *, layer, tr, stride):
    bsz, seq, _ = x1.shape
    nt = seq // tr
    reps = 1 if mod.shape[1] == 1 else tr // mod.shape[1]
    final = final_g is not None
    row = lambda w: pl.BlockSpec((1, tr, w), lambda b, t: (b, t, 0))
    in_specs = [row(D_MODEL), row(D_MODEL),
                pl.BlockSpec((1,) + mod.shape[1:], lambda b, t: (b, 0, 0)),
                _layer_spec(wu, layer), _const_spec(cw.shape), _const_spec(cbias.shape), _layer_spec(wd, layer)]
    args = [x1, h2, mod, wu, cw, cbias, wd]
    if stride == 1:
        state_shape = jax.ShapeDtypeStruct((bsz, 2, 2 * D_FF), _F32)
        state_spec = pl.BlockSpec((1, 2, 2 * D_FF), lambda b, t: (b, 0, 0))
        scratch = [pltpu.VMEM((tr + 8, 2 * D_FF), _F32)]
    else:
        assert bsz == 1 and nt == 1 and tr % stride == 0
        in_specs.append(_const_spec(prev.shape))
        args.append(prev)
        state_shape = jax.ShapeDtypeStruct(prev.shape, _F32)
        state_spec = pl.BlockSpec(prev.shape, lambda b, t: (0, 0))
        scratch = []
    if final:
        in_specs.append(_const_spec(final_g.shape))
        args.append(final_g)
    return pl.pallas_call(
        functools.partial(_ffn_kernel, reps=reps, stride=stride, final=final),
        grid=(bsz, nt),
        in_specs=in_specs,
        out_specs=[row(D_MODEL), state_spec],
        out_shape=[jax.ShapeDtypeStruct(x1.shape, _F32), state_shape],
        scratch_shapes=scratch,
        compiler_params=_params(("arbitrary", "arbitrary")),
        name="conv_ffn",
    )(*args)


def kernel(x_prompt, x_sample, c_prompt, c_sample, cache_k, cache_v, state_conv, state_ffn, page_table,
           rel_bias, ada_w, ada_b, norm_mix_g, norm_ffn_g, final_norm_g, w_in, conv_w, w_att_out,
           w_conv_out, w_o, w_up, ffn_conv_w, ffn_conv_b, w_down):
    depth = w_in.shape[0]
    bsz, seq, _ = x_prompt.shape
    n_seq, n_q, _ = x_sample.shape
    n_pages = page_table.shape[1]
    past_len = n_pages * PAGE_SIZE
    assert seq % BLOCK == 0 and past_len % BLOCK == 0 and n_q <= 8 and bsz <= 8
    assert cache_k.shape[2] == PAGE_SIZE

    pad_rows = 8
    c_all = jnp.concatenate([c_prompt, jnp.zeros((pad_rows - bsz, D_MODEL), _F32), c_sample], axis=0)
    mod_all = _modulation(c_all, ada_w, ada_b)
    mod_p = mod_all[:, :bsz].reshape(depth, bsz, 1, -1)
    mod_s = mod_all[:, pad_rows:].reshape(depth, 1, n_seq, -1)

    bias_p, bias_s8 = _bias_tables(rel_bias, past_len)
    bias_s = bias_s8[:, :n_q].transpose(1, 0, 2).reshape(n_q * N_HEADS, -1)

    page_t = lambda c: c.transpose(0, 1, 3, 4, 2).reshape(c.shape[0], c.shape[1], ATT_WIDTH, PAGE_SIZE)
    cache_kt, cache_vt = page_t(cache_k), page_t(cache_v)

    time_major = lambda a: a.transpose(1, 0, 2).reshape(-1, a.shape[2])
    seq_major = lambda a, steps: a.reshape(steps, n_seq, -1).transpose(1, 0, 2)
    rows_s = n_q * n_seq

    w_in_bf = w_in.astype(_BF)
    wa, wc, wo = w_att_out.astype(_BF), w_conv_out.astype(_BF), w_o.astype(_BF)
    wu, wd = w_up.astype(_BF), w_down.astype(_BF)
    conv_prev = state_conv.transpose(0, 2, 1, 3).reshape(depth, 2 * n_seq, -1)
    ffn_prev = state_ffn.transpose(0, 2, 1, 3).reshape(depth, 2 * n_seq, -1)
    by_seq = lambda a: a.reshape(n_q, n_seq, 1, -1)

    xp = x_prompt
    xs = time_major(x_sample)[None]
    fin = final_norm_g.reshape(1, D_MODEL)
    outs = [[] for _ in range(8)]
    for l in range(depth):
        g_mix = norm_mix_g[l].reshape(1, D_MODEL)
        g_ffn = norm_ffn_g[l].reshape(1, D_MODEL)
        cbias = ffn_conv_b[l].reshape(1, -1)
        final_g = fin if l == depth - 1 else None

        k, v, kx, qtz, qt, vt, kmean, cb, ga, gb, cstate = _inproj_prompt(
            xp, mod_p[l], g_mix, w_in_bf, conv_w[l], l)
        att_t = _attn_prompt(qtz, qt, kx, vt, kmean, bias_p)
        x1, h2 = _post(xp, att_t, cb, ga, gb, mod_p[l], g_ffn, wa, wc, wo,
                       layer=l, tr=2 * BLOCK, att_transposed=True)
        xp, fstate = _ffn(x1, h2, mod_p[l], wu, ffn_conv_w[l], cbias, wd, None, final_g,
                          layer=l, tr=BLOCK, stride=1)
        for dst, val in zip(outs[:4], (k, v, cstate, fstate)):
            dst.append(val)

        q_s, k_s, v_s, cb_s, ga_s, gb_s, cstate_s = _inproj_sample(
            xs[0], mod_s[l], g_mix, w_in_bf, conv_w[l], conv_prev[l], n_q, l)
        att_s = _attn_sample(page_table, by_seq(q_s), by_seq(k_s), by_seq(v_s), bias_s, cache_kt, cache_vt, l)
        x1s, h2s = _post(xs, att_s.reshape(1, rows_s, ATT_WIDTH), cb_s[None], ga_s[None], gb_s[None],
                         mod_s[l], g_ffn, wa, wc, wo, layer=l, tr=rows_s, att_transposed=False)
        xs, fstate_s = _ffn(x1s, h2s, mod_s[l], wu, ffn_conv_w[l], cbias, wd, ffn_prev[l], final_g,
                            layer=l, tr=rows_s, stride=n_seq)
        for dst, val in zip(outs[4:], (k_s, v_s, cstate_s, fstate_s)):
            dst.append(val)

    kp, vp, cp, fp, ks, vs, cs, fs = (jnp.stack(o) for o in outs)
    heads = lambda a: a.reshape(a.shape[:-1] + (N_HEADS, HEAD_DIM))
    to_seq_major = lambda a, steps: a.reshape(depth, steps, n_seq, -1).transpose(0, 2, 1, 3)
    return (xp, seq_major(xs[0], n_q), heads(kp), heads(vp), cp, fp,
            heads(to_seq_major(ks, n_q)), heads(to_seq_major(vs, n_q)), to_seq_major(cs, 2), to_seq_major(fs, 2))
```
